```python
import jax, jax.numpy as jnp
from jax import lax
import numpy as np

D_MODEL = 1024
BATCH = 4
SEQ = 8192
DEPTH = 2

N_MIXERS = 2
N_META = 16
EXPAND = 2
E_POOL = EXPAND * D_MODEL
POOL_WINDOWS = (2, 4, 8, 16)
N_POOL_GROUPS = len(POOL_WINDOWS)
POOL_GROUP_DIM = E_POOL // N_POOL_GROUPS
E_HGRN = EXPAND * D_MODEL
HGRN_HEAD_DIM = 128
N_HGRN_HEADS = E_HGRN // HGRN_HEAD_DIM
CHUNK = 64
PAD_FRONT = (-N_META) % CHUNK
N_POOL_LAYERS = (DEPTH + 1) // 2
N_HGRN_LAYERS = DEPTH // 2
EPS = 1e-6

kernel_name = "hybrid_pool_hgrn2_meta"


def rmsnorm(x, w):
    x32 = x.astype(jnp.float32)
    y = x32 * lax.rsqrt(jnp.mean(x32 * x32, axis=-1, keepdims=True) + EPS)
    return (y * w.astype(jnp.float32)).astype(x.dtype)


def pool_mixer(h, w_in, w_grp, scale, w_out):
    b_, L, _ = h.shape
    v, gate = jnp.split(h @ w_in, 2, axis=-1)
    v32 = v.astype(jnp.float32)
    cs = jnp.cumsum(v32, axis=1)
    t = jnp.arange(L)
    groups = []
    for g, w in enumerate(POOL_WINDOWS):
        sl = slice(g * POOL_GROUP_DIM, (g + 1) * POOL_GROUP_DIM)
        c = cs[..., sl]
        c_shift = jnp.pad(c, ((0, 0), (w, 0), (0, 0)))[:, :L]
        cnt = jnp.minimum(t + 1, w).astype(jnp.float32)[None, :, None]
        groups.append((c - c_shift) / cnt - v32[..., sl])
    u = jnp.stack(groups, axis=2).astype(h.dtype)
    u = jnp.einsum('blgc,gcd->blgd', u, w_grp).reshape(b_, L, E_POOL) * scale
    return (u * jax.nn.silu(gate)) @ w_out


def _hgrn_chunk_step(S, xs):
    q, k, v, g = xs
    c = q.shape[2]
    causal = jnp.tril(jnp.ones((c, c), dtype=bool))
    bcum = jnp.cumsum(g, axis=2)
    rel = bcum[:, :, :, None, :] - bcum[:, :, None, :, :]
    decay = jnp.exp(jnp.where(causal[:, :, None], rel, -jnp.inf))
    scores = jnp.einsum('bhtk,bhsk,bhtsk->bhts', q, k, decay)
    o = jnp.einsum('bhts,bhsv->bhtv', scores, v) \
        + jnp.einsum('bhtk,bhkv->bhtv', q * jnp.exp(bcum), S)
    b_end = bcum[:, :, -1, :]
    S_new = jnp.exp(b_end)[..., None] * S \
        + jnp.einsum('bhsk,bhsv->bhkv', k * jnp.exp(b_end[:, :, None, :] - bcum), v)
    return S_new, o


def hgrn2_mixer(h, w_in, lb, o_norm, w_out):
    b_, L, _ = h.shape
    q, fp, i_in, gate = jnp.split(h @ w_in, 4, axis=-1)
    q = jax.nn.silu(q)
    fp32 = fp.astype(jnp.float32)
    lb32 = lb.astype(jnp.float32)
    f = lb32 + (1.0 - lb32) * jax.nn.sigmoid(fp32)
    log_f = jnp.log(f)
    k = (1.0 - lb32) * jax.nn.sigmoid(-fp32)

    def to_chunks(a):
        a = jnp.pad(a.astype(jnp.float32), ((0, 0), (PAD_FRONT, 0), (0, 0)))
        n = a.shape[1] // CHUNK
        return a.reshape(b_, n, CHUNK, N_HGRN_HEADS, HGRN_HEAD_DIM).transpose(1, 0, 3, 2, 4)

    xs = (to_chunks(q), to_chunks(k), to_chunks(i_in), to_chunks(log_f))
    S0 = jnp.zeros((b_, N_HGRN_HEADS, HGRN_HEAD_DIM, HGRN_HEAD_DIM), jnp.float32)
    _, o = lax.scan(_hgrn_chunk_step, S0, xs)
    n = o.shape[0]
    o = o.transpose(1, 0, 3, 2, 4).reshape(b_, n * CHUNK, N_HGRN_HEADS, HGRN_HEAD_DIM)[:, PAD_FRONT:]
    o = o * lax.rsqrt(jnp.mean(o * o, axis=-1, keepdims=True) + EPS)
    o = o.reshape(b_, L, E_HGRN) * o_norm.astype(jnp.float32)
    return (o.astype(h.dtype) * jax.nn.silu(gate)) @ w_out


def setup_inputs(seed: int = 0) -> dict:
    key = jax.random.key(seed)
    ks = jax.random.split(key, 12)
    nrm = jax.random.normal
    f32 = jnp.float32
    return {
        "x": nrm(ks[0], (BATCH, SEQ, D_MODEL), f32),
        "meta_tokens": nrm(ks[1], (N_META, D_MODEL), f32),
        "norm_w": 1.0 + 0.05 * nrm(ks[2], (DEPTH, D_MODEL), f32),
        "pool_w_in": nrm(ks[3], (N_POOL_LAYERS, D_MODEL, 2 * E_POOL), f32) * D_MODEL ** -0.5,
        "pool_w_grp": nrm(ks[4], (N_POOL_LAYERS, N_POOL_GROUPS, POOL_GROUP_DIM, POOL_GROUP_DIM), f32) * POOL_GROUP_DIM ** -0.5,
        "pool_scale": 1.0 + 0.1 * nrm(ks[5], (N_POOL_LAYERS, E_POOL), f32),
        "pool_w_out": nrm(ks[6], (N_POOL_LAYERS, E_POOL, D_MODEL), f32) * E_POOL ** -0.5,
        "hgrn_w_in": nrm(ks[7], (N_HGRN_LAYERS, D_MODEL, 4 * E_HGRN), f32) * D_MODEL ** -0.5,
        "hgrn_lb_logits": 1.0 + 0.5 * nrm(ks[8], (DEPTH, E_HGRN), f32),
        "hgrn_o_norm": 1.0 + 0.05 * nrm(ks[9], (N_HGRN_LAYERS, E_HGRN), f32),
        "hgrn_w_out": nrm(ks[10], (N_HGRN_LAYERS, E_HGRN, D_MODEL), f32) * E_HGRN ** -0.5,
        "final_norm_w": 1.0 + 0.05 * nrm(ks[11], (D_MODEL,), f32),
    }


def reference(x, meta_tokens, norm_w, pool_w_in, pool_w_grp, pool_scale, pool_w_out,
              hgrn_w_in, hgrn_lb_logits, hgrn_o_norm, hgrn_w_out, final_norm_w):
    b_ = x.shape[0]
    meta = jnp.broadcast_to(meta_tokens.astype(x.dtype)[None], (b_, N_META, x.shape[-1]))
    h = jnp.concatenate([meta, x], axis=1)
    p = jax.nn.softmax(hgrn_lb_logits.astype(jnp.float32), axis=0)
    lb_all = jnp.cumsum(p, axis=0) - p[0]
    for i in range(DEPTH):
        hn = rmsnorm(h, norm_w[i])
        j = i // N_MIXERS
        if i % N_MIXERS == 0:
            y = pool_mixer(hn, pool_w_in[j], pool_w_grp[j], pool_scale[j], pool_w_out[j])
        else:
            y = hgrn2_mixer(hn, hgrn_w_in[j], lb_all[i], hgrn_o_norm[j], hgrn_w_out[j])
        h = h + y
    return rmsnorm(h[:, N_META:], final_norm_w)
```

```python
import functools

import jax
import jax.numpy as jnp
from jax import lax
from jax.experimental import pallas as pl
from jax.experimental.pallas import tpu as pltpu

D_MODEL = 1024
N_META = 16
E_POOL = 2048
POOL_WINDOWS = (2, 4, 8, 16)
POOL_GROUP_DIM = E_POOL // len(POOL_WINDOWS)
MAX_WINDOW = max(POOL_WINDOWS)
E_HGRN = 2048
HEAD_DIM = 128
N_HEADS = E_HGRN // HEAD_DIM
EPS = 1e-6

ROW_TILE = 256
PAD_ROWS = ROW_TILE - N_META
CHUNK = 64
HALF = CHUNK // 2
VMEM_LIMIT_BYTES = 56 * 1024 * 1024

_F32 = jnp.float32
_BF16 = jnp.bfloat16


def _rmsnorm(x, w):
    ms = jnp.mean(x * x, axis=-1, keepdims=True)
    return x * lax.rsqrt(ms + EPS) * w


def _silu(x):
    return x * jax.nn.sigmoid(x)


def _dot(a, b):
    return jnp.dot(a, b, preferred_element_type=_F32)


def _dot_nt(a, b):
    return lax.dot_general(a, b, (((1,), (1,)), ((), ())), preferred_element_type=_F32)


def _dot_tn(a, b):
    return lax.dot_general(a, b, (((0,), (0,)), ((), ())), preferred_element_type=_F32)


def _pool_layer_kernel(meta_ref, x_ref, nw_ref, win_ref, wgrp_ref, scale_ref, wout_ref, out_ref, vbuf):
    i = pl.program_id(1)

    @pl.when(i == 0)
    def _():
        vbuf[0:MAX_WINDOW, :] = jnp.zeros((MAX_WINDOW, E_POOL), _F32)

    h = jnp.where(i == 0, meta_ref[...], x_ref[0])
    hn = _rmsnorm(h, nw_ref[...]).astype(_BF16)
    v = _dot(hn, win_ref[:, 0:E_POOL])
    gate = _dot(hn, win_ref[:, E_POOL:2 * E_POOL])
    vbuf[MAX_WINDOW:MAX_WINDOW + ROW_TILE, :] = v

    pos = i * ROW_TILE - PAD_ROWS + lax.broadcasted_iota(jnp.int32, (ROW_TILE, 1), 0)

    parts = []
    for g, w in enumerate(POOL_WINDOWS):
        c0 = g * POOL_GROUP_DIM
        cols = slice(c0, c0 + POOL_GROUP_DIM)

        def rows(shift, cols=cols):
            return vbuf[MAX_WINDOW - shift:MAX_WINDOW - shift + ROW_TILE, cols]

        if w > 8:
            acc = rows(0) + rows(8)
            for j in range(1, 8):
                acc = acc + (rows(j) + rows(j + 8))
        else:
            acc = rows(0)
            for j in range(1, w):
                acc = acc + rows(j)
        cnt = jnp.clip(pos + 1, 1, w).astype(_F32)
        u = acc * (1.0 / cnt) - rows(0)
        parts.append(_dot(u.astype(_BF16), wgrp_ref[g]))
    u = jnp.concatenate(parts, axis=-1) * scale_ref[...]
    y = _dot((u * _silu(gate)).astype(_BF16), wout_ref[...])
    out_ref[0] = h + y

    vbuf[0:MAX_WINDOW, :] = vbuf[ROW_TILE:ROW_TILE + MAX_WINDOW, :]


def _hgrn_layer_kernel(layer, h_ref, nw_ref, win_ref, lbl_ref, onw_ref, wout_ref, fw_ref, out_ref,
                       q_s, k_s, v_s, g_s, gate_s, o_s, st_s):
    i = pl.program_id(1)

    @pl.when(i == 0)
    def _():
        st_s[...] = jnp.zeros(st_s.shape, _F32)

    logits = lbl_ref[...]
    e = jnp.exp(logits - jnp.max(logits, axis=0, keepdims=True))
    p = e / jnp.sum(e, axis=0, keepdims=True)
    lb = jnp.sum(p[0:layer + 1], axis=0, keepdims=True) - p[0:1]

    h = h_ref[0]
    hn = _rmsnorm(h, nw_ref[...]).astype(_BF16)
    q_s[...] = _silu(_dot(hn, win_ref[:, 0:E_HGRN]))
    fp = _dot(hn, win_ref[:, E_HGRN:2 * E_HGRN])
    t = jnp.exp(-jnp.abs(fp))
    r = 1.0 / (1.0 + t)
    tr = t * r
    sig_pos = jnp.where(fp >= 0, r, tr)
    sig_neg = jnp.where(fp >= 0, tr, r)
    g_s[...] = jnp.log(lb + (1.0 - lb) * sig_pos)
    k_s[...] = (1.0 - lb) * sig_neg
    v_s[...] = _dot(hn, win_ref[:, 2 * E_HGRN:3 * E_HGRN]).astype(_BF16)
    gate_s[...] = _silu(_dot(hn, win_ref[:, 3 * E_HGRN:4 * E_HGRN]))

    row = lax.broadcasted_iota(jnp.int32, (CHUNK, CHUNK), 0)
    col = lax.broadcasted_iota(jnp.int32, (CHUNK, CHUNK), 1)
    causal = col <= row
    tri = causal.astype(_BF16)

    def chunk_body(c, carry):
        r0 = pl.multiple_of(c * CHUNK, CHUNK)
        rs = pl.ds(r0, CHUNK)
        g = g_s[rs, :]
        g_hi = g.astype(_BF16)
        g_lo = (g - g_hi.astype(_F32)).astype(_BF16)
        b = _dot(tri, g_hi) + _dot(tri, g_lo)
        b_mid = b[HALF - 1:HALF, :]
        b_end = b[CHUNK - 1:CHUNK, :]
        q = q_s[rs, :]
        k = k_s[rs, :]
        q_in = (q * jnp.exp(b - b_mid)).astype(_BF16)
        k_in = (k * jnp.exp(b_mid - b)).astype(_BF16)
        q_st = (q * jnp.exp(b)).astype(_BF16)
        k_st = (k * jnp.exp(b_end - b)).astype(_BF16)
        d_end = jnp.exp(b_end)
        v = v_s[rs, :]
        for hd in range(N_HEADS):
            ls = slice(hd * HEAD_DIM, (hd + 1) * HEAD_DIM)
            a = jnp.where(causal, _dot_nt(q_in[:, ls], k_in[:, ls]), 0.0).astype(_BF16)
            st = st_s[hd]
            o = _dot(a, v[:, ls]) + _dot_nt(q_st[:, ls], st.astype(_BF16))
            o_s[rs, ls] = o
            st_s[hd] = st * d_end[:, ls] + _dot_tn(v[:, ls], k_st[:, ls])
        return carry

    lax.fori_loop(0, ROW_TILE // CHUNK, chunk_body, 0)

    parts = []
    for hd in range(N_HEADS):
        ls = slice(hd * HEAD_DIM, (hd + 1) * HEAD_DIM)
        o = o_s[:, ls]
        parts.append(o * lax.rsqrt(jnp.mean(o * o, axis=-1, keepdims=True) + EPS))
    o = jnp.concatenate(parts, axis=-1) * onw_ref[...]
    y = _dot((o * gate_s[...]).astype(_BF16), wout_ref[...])
    out_ref[0] = _rmsnorm(h + y, fw_ref[...])


def _const_spec(shape):
    zeros = (0,) * len(shape)
    return pl.BlockSpec(shape, lambda b, i: zeros, pipeline_mode=pl.Buffered(1))


def kernel(x, meta_tokens, norm_w, pool_w_in, pool_w_grp, pool_scale, pool_w_out, hgrn_w_in, hgrn_lb_logits,
           hgrn_o_norm, hgrn_w_out, final_norm_w):
    batch, seq, d = x.shape
    depth = norm_w.shape[0]
    assert d == D_MODEL and seq % ROW_TILE == 0 and depth == 2
    assert meta_tokens.shape == (N_META, D_MODEL)
    n_tiles = seq // ROW_TILE + 1
    grid = (batch, n_tiles)
    params = pltpu.CompilerParams(dimension_semantics=("arbitrary", "arbitrary"),
                                  vmem_limit_bytes=VMEM_LIMIT_BYTES)

    meta_pad = jnp.concatenate([jnp.zeros((PAD_ROWS, D_MODEL), x.dtype), meta_tokens.astype(x.dtype)], axis=0)
    row_spec = pl.BlockSpec((1, ROW_TILE, D_MODEL), lambda b, i: (b, i, 0))
    x_spec = pl.BlockSpec((1, ROW_TILE, D_MODEL), lambda b, i: (b, jnp.maximum(i - 1, 0), 0))

    h1 = pl.pallas_call(
        _pool_layer_kernel,
        grid=grid,
        in_specs=[
            _const_spec((ROW_TILE, D_MODEL)),
            x_spec,
            _const_spec((1, D_MODEL)),
            _const_spec((D_MODEL, 2 * E_POOL)),
            _const_spec((len(POOL_WINDOWS), POOL_GROUP_DIM, POOL_GROUP_DIM)),
            _const_spec((1, E_POOL)),
            _const_spec((E_POOL, D_MODEL)),
        ],
        out_specs=row_spec,
        out_shape=jax.ShapeDtypeStruct((batch, n_tiles * ROW_TILE, D_MODEL), _F32),
        scratch_shapes=[pltpu.VMEM((ROW_TILE + MAX_WINDOW, E_POOL), _F32)],
        compiler_params=params,
        name="pool_layer",
    )(meta_pad, x, norm_w[0:1], pool_w_in[0].astype(_BF16), pool_w_grp[0].astype(_BF16), pool_scale[0:1],
      pool_w_out[0].astype(_BF16))

    out = pl.pallas_call(
        functools.partial(_hgrn_layer_kernel, 1),
        grid=grid,
        in_specs=[
            row_spec,
            _const_spec((1, D_MODEL)),
            _const_spec((D_MODEL, 4 * E_HGRN)),
            _const_spec((depth, E_HGRN)),
            _const_spec((1, E_HGRN)),
            _const_spec((E_HGRN, D_MODEL)),
            _const_spec((1, D_MODEL)),
        ],
        out_specs=x_spec,
        out_shape=jax.ShapeDtypeStruct((batch, seq, D_MODEL), _F32),
        scratch_shapes=[
            pltpu.VMEM((ROW_TILE, E_HGRN), _F32),
            pltpu.VMEM((ROW_TILE, E_HGRN), _F32),
            pltpu.VMEM((ROW_TILE, E_HGRN), _BF16),
            pltpu.VMEM((ROW_TILE, E_HGRN), _F32),
            pltpu.VMEM((ROW_TILE, E_HGRN), _F32),
            pltpu.VMEM((ROW_TILE, E_HGRN), _F32),
            pltpu.VMEM((N_HEADS, HEAD_DIM, HEAD_DIM), _F32),
        ],
        compiler_params=params,
        name="hgrn_layer",
    )(h1, norm_w[1:2], hgrn_w_in[0].astype(_BF16), hgrn_lb_logits, hgrn_o_norm[0:1], hgrn_w_out[0].astype(_BF16),
      final_norm_w.reshape(1, D_MODEL))
    return out
```

```python
import functools

import jax
import jax.numpy as jnp
from jax import lax
from jax.experimental import pallas as pl
from jax.experimental.pallas import tpu as pltpu

D_MODEL = 1024
N_META = 16
E_POOL = 2048
POOL_WINDOWS = (2, 4, 8, 16)
POOL_GROUP_DIM = E_POOL // len(POOL_WINDOWS)
MAX_WINDOW = max(POOL_WINDOWS)
E_HGRN = 2048
HEAD_DIM = 128
N_HEADS = E_HGRN // HEAD_DIM
EPS = 1e-6

ROW_TILE = 256
PAD_ROWS = ROW_TILE - N_META
CHUNK = 128
HALF = CHUNK // 2
VMEM_LIMIT_BYTES = 56 * 1024 * 1024

_F32 = jnp.float32
_BF16 = jnp.bfloat16


def _rmsnorm(x, w):
    ms = jnp.mean(x * x, axis=-1, keepdims=True)
    return x * lax.rsqrt(ms + EPS) * w


def _silu(x):
    return x * jax.nn.sigmoid(x)


def _dot(a, b):
    return jnp.dot(a, b, preferred_element_type=_F32)


def _dot_nt(a, b):
    return lax.dot_general(a, b, (((1,), (1,)), ((), ())), preferred_element_type=_F32)


def _dot_tn(a, b):
    return lax.dot_general(a, b, (((0,), (0,)), ((), ())), preferred_element_type=_F32)


def _pool_layer_kernel(meta_ref, x_ref, nw_ref, win_ref, wgrp_ref, scale_ref, wout_ref, out_ref, vbuf):
    i = pl.program_id(1)

    @pl.when(i == 0)
    def _():
        vbuf[0:MAX_WINDOW, :] = jnp.zeros((MAX_WINDOW, E_POOL), _F32)

    h = jnp.where(i == 0, meta_ref[...], x_ref[0])
    hn = _rmsnorm(h, nw_ref[...]).astype(_BF16)
    v = _dot(hn, win_ref[:, 0:E_POOL])
    gate = _dot(hn, win_ref[:, E_POOL:2 * E_POOL])
    vbuf[MAX_WINDOW:MAX_WINDOW + ROW_TILE, :] = v

    pos = i * ROW_TILE - PAD_ROWS + lax.broadcasted_iota(jnp.int32, (ROW_TILE, 1), 0)

    parts = []
    for g, w in enumerate(POOL_WINDOWS):
        c0 = g * POOL_GROUP_DIM
        cols = slice(c0, c0 + POOL_GROUP_DIM)

        def rows(shift, cols=cols):
            return vbuf[MAX_WINDOW - shift:MAX_WINDOW - shift + ROW_TILE, cols]

        if w > 8:
            acc = rows(0) + rows(8)
            for j in range(1, 8):
                acc = acc + (rows(j) + rows(j + 8))
        else:
            acc = rows(0)
            for j in range(1, w):
                acc = acc + rows(j)
        cnt = jnp.clip(pos + 1, 1, w).astype(_F32)
        u = acc * (1.0 / cnt) - rows(0)
        parts.append(_dot(u.astype(_BF16), wgrp_ref[g]))
    u = jnp.concatenate(parts, axis=-1) * scale_ref[...]
    y = _dot((u * _silu(gate)).astype(_BF16), wout_ref[...])
    out_ref[0] = h + y

    vbuf[0:MAX_WINDOW, :] = vbuf[ROW_TILE:ROW_TILE + MAX_WINDOW, :]


def _hgrn_layer_kernel(layer, h_ref, nw_ref, win_ref, lbl_ref, onw_ref, wout_ref, fw_ref, out_ref,
                       q_s, k_s, v_s, g_s, gate_s, qin_s, kin_s, qst_s, kst_s, a_s, o_s, st_s):
    i = pl.program_id(1)

    @pl.when(i == 0)
    def _():
        st_s[...] = jnp.zeros(st_s.shape, _F32)

    logits = lbl_ref[...]
    e = jnp.exp(logits - jnp.max(logits, axis=0, keepdims=True))
    p = e / jnp.sum(e, axis=0, keepdims=True)
    lb = jnp.sum(p[0:layer + 1], axis=0, keepdims=True) - p[0:1]

    h = h_ref[0]
    hn = _rmsnorm(h, nw_ref[...]).astype(_BF16)
    q_s[...] = _silu(_dot(hn, win_ref[:, 0:E_HGRN]))
    fp = _dot(hn, win_ref[:, E_HGRN:2 * E_HGRN])
    t = jnp.exp(-jnp.abs(fp))
    r = 1.0 / (1.0 + t)
    tr = t * r
    sig_pos = jnp.where(fp >= 0, r, tr)
    sig_neg = jnp.where(fp >= 0, tr, r)
    g_s[...] = jnp.log(lb + (1.0 - lb) * sig_pos)
    k_s[...] = (1.0 - lb) * sig_neg
    v_s[...] = _dot(hn, win_ref[:, 2 * E_HGRN:3 * E_HGRN]).astype(_BF16)
    gate_s[...] = _silu(_dot(hn, win_ref[:, 3 * E_HGRN:4 * E_HGRN]))

    row = lax.broadcasted_iota(jnp.int32, (CHUNK, CHUNK), 0)
    col = lax.broadcasted_iota(jnp.int32, (CHUNK, CHUNK), 1)
    causal = col <= row
    tri = causal.astype(_BF16)
    tri2 = jnp.concatenate([tri, tri], axis=1)

    d_end = []
    for c in range(ROW_TILE // CHUNK):
        rs = slice(c * CHUNK, (c + 1) * CHUNK)
        g = g_s[rs, :]
        g_hi = g.astype(_BF16)
        g_lo = (g - g_hi.astype(_F32)).astype(_BF16)
        b = _dot(tri2, jnp.concatenate([g_hi, g_lo], axis=0))
        b_mid = b[HALF - 1:HALF, :]
        b_end = b[CHUNK - 1:CHUNK, :]
        q = q_s[rs, :]
        k = k_s[rs, :]
        qin_s[rs, :] = (q * jnp.exp(b - b_mid)).astype(_BF16)
        kin_s[rs, :] = (k * jnp.exp(b_mid - b)).astype(_BF16)
        qst_s[rs, :] = (q * jnp.exp(b)).astype(_BF16)
        kst_s[rs, :] = (k * jnp.exp(b_end - b)).astype(_BF16)
        d_end.append(jnp.exp(b_end))

    for c in range(ROW_TILE // CHUNK):
        rs = slice(c * CHUNK, (c + 1) * CHUNK)
        for hd in range(N_HEADS):
            ls = slice(hd * HEAD_DIM, (hd + 1) * HEAD_DIM)
            a = _dot_nt(qin_s[rs, ls], kin_s[rs, ls])
            a_s[rs, ls] = jnp.where(causal, a, 0.0).astype(_BF16)

    for c in range(ROW_TILE // CHUNK):
        rs = slice(c * CHUNK, (c + 1) * CHUNK)
        for hd in range(N_HEADS):
            ls = slice(hd * HEAD_DIM, (hd + 1) * HEAD_DIM)
            st = st_s[hd]
            v = v_s[rs, ls]
            lhs = jnp.concatenate([qst_s[rs, ls], a_s[rs, ls]], axis=1)
            rhs = jnp.concatenate([st.astype(_BF16), v], axis=0)
            o_s[rs, ls] = _dot(lhs, rhs)
            d_col = jnp.transpose(jnp.broadcast_to(d_end[c][:, ls], (HEAD_DIM, HEAD_DIM)))
            st_s[hd] = st * d_col + _dot_tn(kst_s[rs, ls], v)

    parts = []
    for hd in range(N_HEADS):
        ls = slice(hd * HEAD_DIM, (hd + 1) * HEAD_DIM)
        o = o_s[:, ls]
        parts.append(o * lax.rsqrt(jnp.mean(o * o, axis=-1, keepdims=True) + EPS))
    o = jnp.concatenate(parts, axis=-1) * onw_ref[...]
    y = _dot((o * gate_s[...]).astype(_BF16), wout_ref[...])
    out_ref[0] = _rmsnorm(h + y, fw_ref[...])


def _const_spec(shape):
    zeros = (0,) * len(shape)
    return pl.BlockSpec(shape, lambda b, i: zeros, pipeline_mode=pl.Buffered(1))


def kernel(x, meta_tokens, norm_w, pool_w_in, pool_w_grp, pool_scale, pool_w_out, hgrn_w_in, hgrn_lb_logits,
           hgrn_o_norm, hgrn_w_out, final_norm_w):
    batch, seq, d = x.shape
    depth = norm_w.shape[0]
    assert d == D_MODEL and seq % ROW_TILE == 0 and depth == 2
    assert meta_tokens.shape == (N_META, D_MODEL)
    n_tiles = seq // ROW_TILE + 1
    grid = (batch, n_tiles)
    params = pltpu.CompilerParams(dimension_semantics=("arbitrary", "arbitrary"),
                                  vmem_limit_bytes=VMEM_LIMIT_BYTES)

    meta_pad = jnp.concatenate([jnp.zeros((PAD_ROWS, D_MODEL), x.dtype), meta_tokens.astype(x.dtype)], axis=0)
    row_spec = pl.BlockSpec((1, ROW_TILE, D_MODEL), lambda b, i: (b, i, 0))
    x_spec = pl.BlockSpec((1, ROW_TILE, D_MODEL), lambda b, i: (b, jnp.maximum(i - 1, 0), 0))

    h1 = pl.pallas_call(
        _pool_layer_kernel,
        grid=grid,
        in_specs=[
            _const_spec((ROW_TILE, D_MODEL)),
            x_spec,
            _const_spec((1, D_MODEL)),
            _const_spec((D_MODEL, 2 * E_POOL)),
            _const_spec((len(POOL_WINDOWS), POOL_GROUP_DIM, POOL_GROUP_DIM)),
            _const_spec((1, E_POOL)),
            _const_spec((E_POOL, D_MODEL)),
        ],
        out_specs=row_spec,
        out_shape=jax.ShapeDtypeStruct((batch, n_tiles * ROW_TILE, D_MODEL), _F32),
        scratch_shapes=[pltpu.VMEM((ROW_TILE + MAX_WINDOW, E_POOL), _F32)],
        compiler_params=params,
        name="pool_layer",
    )(meta_pad, x, norm_w[0:1], pool_w_in[0].astype(_BF16), pool_w_grp[0].astype(_BF16), pool_scale[0:1],
      pool_w_out[0].astype(_BF16))

    out = pl.pallas_call(
        functools.partial(_hgrn_layer_kernel, 1),
        grid=grid,
        in_specs=[
            row_spec,
            _const_spec((1, D_MODEL)),
            _const_spec((D_MODEL, 4 * E_HGRN)),
            _const_spec((depth, E_HGRN)),
            _const_spec((1, E_HGRN)),
            _const_spec((E_HGRN, D_MODEL)),
            _const_spec((1, D_MODEL)),
        ],
        out_specs=x_spec,
        out_shape=jax.ShapeDtypeStruct((batch, seq, D_MODEL), _F32),
        scratch_shapes=[
            pltpu.VMEM((ROW_TILE, E_HGRN), _F32),
            pltpu.VMEM((ROW_TILE, E_HGRN), _F32),
            pltpu.VMEM((ROW_TILE, E_HGRN), _BF16),
            pltpu.VMEM((ROW_TILE, E_HGRN), _F32),
            pltpu.VMEM((ROW_TILE, E_HGRN), _F32),
            pltpu.VMEM((ROW_TILE, E_HGRN), _BF16),
            pltpu.VMEM((ROW_TILE, E_HGRN), _BF16),
            pltpu.VMEM((ROW_TILE, E_HGRN), _BF16),
            pltpu.VMEM((ROW_TILE, E_HGRN), _BF16),
            pltpu.VMEM((ROW_TILE, E_HGRN), _BF16),
            pltpu.VMEM((ROW_TILE, E_HGRN), _F32),
            pltpu.VMEM((N_HEADS, HEAD_DIM, HEAD_DIM), _F32),
        ],
        compiler_params=params,
        name="hgrn_layer",
    )(h1, norm_w[1:2], hgrn_w_in[0].astype(_BF16), hgrn_lb_logits, hgrn_o_norm[0:1], hgrn_w_out[0].astype(_BF16),
      final_norm_w.reshape(1, D_MODEL))
    return out
```

```python
import functools

import jax
import jax.numpy as jnp
from jax import lax
from jax.experimental import pallas as pl
from jax.experimental.pallas import tpu as pltpu

D_MODEL = 1024
N_META = 16
E_POOL = 2048
POOL_WINDOWS = (2, 4, 8, 16)
POOL_GROUP_DIM = E_POOL // len(POOL_WINDOWS)
MAX_WINDOW = max(POOL_WINDOWS)
E_HGRN = 2048
HEAD_DIM = 128
N_HEADS = E_HGRN // HEAD_DIM
EPS = 1e-6
LOG2_E = 1.4426950408889634

ROW_TILE = 256
PAD_ROWS = ROW_TILE - N_META
CHUNK = 128
HALF = CHUNK // 2
COL_TILE = 256
VMEM_LIMIT_BYTES = 56 * 1024 * 1024

_F32 = jnp.float32
_BF16 = jnp.bfloat16


def _rmsnorm(x, w):
    ms = jnp.mean(x * x, axis=-1, keepdims=True)
    return x * lax.rsqrt(ms + EPS) * w


def _silu(x):
    return x * jax.nn.sigmoid(x)


def _dot(a, b):
    return jnp.dot(a, b, preferred_element_type=_F32)


def _dot_nt(a, b):
    return lax.dot_general(a, b, (((1,), (1,)), ((), ())), preferred_element_type=_F32)


def _dot_tn(a, b):
    return lax.dot_general(a, b, (((0,), (0,)), ((), ())), preferred_element_type=_F32)


def _pool_layer_kernel(meta_ref, x_ref, nw_ref, win_ref, wgrp_ref, scale_ref, wout_ref, out_ref, vbuf):
    i = pl.program_id(1)

    @pl.when(i == 0)
    def _():
        vbuf[0:MAX_WINDOW, :] = jnp.zeros((MAX_WINDOW, E_POOL), _F32)

    h = jnp.where(i == 0, meta_ref[...], x_ref[0])
    hn = _rmsnorm(h, nw_ref[...]).astype(_BF16)
    v = _dot(hn, win_ref[:, 0:E_POOL])
    gate = _dot(hn, win_ref[:, E_POOL:2 * E_POOL])
    vbuf[MAX_WINDOW:MAX_WINDOW + ROW_TILE, :] = v

    pos = i * ROW_TILE - PAD_ROWS + lax.broadcasted_iota(jnp.int32, (ROW_TILE, 1), 0)

    parts = []
    for g, w in enumerate(POOL_WINDOWS):
        c0 = g * POOL_GROUP_DIM
        cols = slice(c0, c0 + POOL_GROUP_DIM)

        def rows(shift, cols=cols):
            return vbuf[MAX_WINDOW - shift:MAX_WINDOW - shift + ROW_TILE, cols]

        if w > 8:
            acc = rows(0) + rows(8)
            for j in range(1, 8):
                acc = acc + (rows(j) + rows(j + 8))
        else:
            acc = rows(0)
            for j in range(1, w):
                acc = acc + rows(j)
        cnt = jnp.clip(pos + 1, 1, w).astype(_F32)
        u = acc * (1.0 / cnt) - rows(0)
        parts.append(_dot(u.astype(_BF16), wgrp_ref[g]))
    u = jnp.concatenate(parts, axis=-1) * scale_ref[...]
    y = _dot((u * _silu(gate)).astype(_BF16), wout_ref[...])
    out_ref[0] = h + y

    vbuf[0:MAX_WINDOW, :] = vbuf[ROW_TILE:ROW_TILE + MAX_WINDOW, :]


def _hgrn_layer_kernel(layer, h_ref, nw_ref, win_ref, lbl_ref, onw_ref, wout_ref, fw_ref, out_ref,
                       q_s, k_s, v_s, b_s, gate_s, qin_s, kin_s, qst_s, kst_s, a_s, o_s, st_s):
    i = pl.program_id(1)

    @pl.when(i == 0)
    def _():
        st_s[...] = jnp.zeros(st_s.shape, _F32)

    logits = lbl_ref[...]
    e = jnp.exp(logits - jnp.max(logits, axis=0, keepdims=True))
    p = e / jnp.sum(e, axis=0, keepdims=True)
    lb = jnp.sum(p[0:layer + 1], axis=0, keepdims=True) - p[0:1]

    h = h_ref[0]
    hn = _rmsnorm(h, nw_ref[...]).astype(_BF16)
    chunks = [slice(c * CHUNK, (c + 1) * CHUNK) for c in range(ROW_TILE // CHUNK)]
    col_tiles = [slice(j * COL_TILE, (j + 1) * COL_TILE) for j in range(E_HGRN // COL_TILE)]

    def proj(part, cs):
        return _dot(hn, win_ref[:, part * E_HGRN + cs.start:part * E_HGRN + cs.stop])

    for cs in col_tiles:
        fp = proj(1, cs)
        t = jnp.exp2(jnp.abs(fp) * (-LOG2_E))
        r = 1.0 / (1.0 + t)
        tr = t * r
        sig_pos = jnp.where(fp >= 0, r, tr)
        sig_neg = jnp.where(fp >= 0, tr, r)
        b_s[:, cs] = jnp.log2(lb[:, cs] + (1.0 - lb[:, cs]) * sig_pos)
        k_s[:, cs] = (1.0 - lb[:, cs]) * sig_neg
    for cs in col_tiles:
        v_s[:, cs] = proj(2, cs).astype(_BF16)
    for cs in col_tiles:
        q_s[:, cs] = _silu(proj(0, cs))

    row = lax.broadcasted_iota(jnp.int32, (CHUNK, CHUNK), 0)
    col = lax.broadcasted_iota(jnp.int32, (CHUNK, CHUNK), 1)
    causal = col <= row
    tri = causal.astype(_BF16)
    tri2 = jnp.concatenate([tri, tri], axis=1)
    for rs in chunks:
        for cs in col_tiles:
            g = b_s[rs, cs]
            g_hi = g.astype(_BF16)
            g_lo = (g - g_hi.astype(_F32)).astype(_BF16)
            b_s[rs, cs] = _dot(tri2, jnp.concatenate([g_hi, g_lo], axis=0))

    for cs in col_tiles:
        gate_s[:, cs] = _silu(proj(3, cs))

    d_end = {}
    for c, rs in enumerate(chunks):
        for j, cs in enumerate(col_tiles):
            b = b_s[rs, cs]
            b_mid = b_s[rs.start + HALF - 1:rs.start + HALF, cs]
            b_end = b_s[rs.stop - 1:rs.stop, cs]
            k = k_s[rs, cs]
            q = q_s[rs, cs]
            kin_s[rs, cs] = (k * jnp.exp2(b_mid - b)).astype(_BF16)
            qin_s[rs, cs] = (q * jnp.exp2(b - b_mid)).astype(_BF16)
            kst_s[rs, cs] = (k * jnp.exp2(b_end - b)).astype(_BF16)
            qst_s[rs, cs] = (q * jnp.exp2(b)).astype(_BF16)
            d_end[c, j] = jnp.exp2(b_end)

    for rs in chunks:
        for hd in range(N_HEADS):
            ls = slice(hd * HEAD_DIM, (hd + 1) * HEAD_DIM)
            a = _dot_nt(qin_s[rs, ls], kin_s[rs, ls])
            a_s[rs, ls] = jnp.where(causal, a, 0.0).astype(_BF16)

    heads_per_tile = COL_TILE // HEAD_DIM
    for c, rs in enumerate(chunks):
        for hd in range(N_HEADS):
            ls = slice(hd * HEAD_DIM, (hd + 1) * HEAD_DIM)
            st = st_s[hd]
            v = v_s[rs, ls]
            lhs = jnp.concatenate([qst_s[rs, ls], a_s[rs, ls]], axis=1)
            rhs = jnp.concatenate([st.astype(_BF16), v], axis=0)
            o_s[rs, ls] = _dot(lhs, rhs)
            d_row = d_end[c, hd // heads_per_tile][:, (hd % heads_per_tile) * HEAD_DIM:][:, :HEAD_DIM]
            d_col = jnp.transpose(jnp.broadcast_to(d_row, (HEAD_DIM, HEAD_DIM)))
            st_s[hd] = st * d_col + _dot_tn(kst_s[rs, ls], v)

    parts = []
    for hd in range(N_HEADS):
        ls = slice(hd * HEAD_DIM, (hd + 1) * HEAD_DIM)
        o = o_s[:, ls]
        on = o * lax.rsqrt(jnp.mean(o * o, axis=-1, keepdims=True) + EPS)
        parts.append((on * onw_ref[:, ls] * gate_s[:, ls]).astype(_BF16))
    y = _dot(jnp.concatenate(parts, axis=-1), wout_ref[...])
    out_ref[0] = _rmsnorm(h + y, fw_ref[...])


def _const_spec(shape):
    zeros = (0,) * len(shape)
    return pl.BlockSpec(shape, lambda b, i: zeros, pipeline_mode=pl.Buffered(1))


def kernel(x, meta_tokens, norm_w, pool_w_in, pool_w_grp, pool_scale, pool_w_out, hgrn_w_in, hgrn_lb_logits,
           hgrn_o_norm, hgrn_w_out, final_norm_w):
    batch, seq, d = x.shape
    depth = norm_w.shape[0]
    assert d == D_MODEL and seq % ROW_TILE == 0 and depth == 2
    assert meta_tokens.shape == (N_META, D_MODEL)
    n_tiles = seq // ROW_TILE + 1
    grid = (batch, n_tiles)
    params = pltpu.CompilerParams(dimension_semantics=("arbitrary", "arbitrary"),
                                  vmem_limit_bytes=VMEM_LIMIT_BYTES)

    meta_pad = jnp.concatenate([jnp.zeros((PAD_ROWS, D_MODEL), x.dtype), meta_tokens.astype(x.dtype)], axis=0)
    row_spec = pl.BlockSpec((1, ROW_TILE, D_MODEL), lambda b, i: (b, i, 0))
    x_spec = pl.BlockSpec((1, ROW_TILE, D_MODEL), lambda b, i: (b, jnp.maximum(i - 1, 0), 0))

    h1 = pl.pallas_call(
        _pool_layer_kernel,
        grid=grid,
        in_specs=[
            _const_spec((ROW_TILE, D_MODEL)),
            x_spec,
            _const_spec((1, D_MODEL)),
            _const_spec((D_MODEL, 2 * E_POOL)),
            _const_spec((len(POOL_WINDOWS), POOL_GROUP_DIM, POOL_GROUP_DIM)),
            _const_spec((1, E_POOL)),
            _const_spec((E_POOL, D_MODEL)),
        ],
        out_specs=row_spec,
        out_shape=jax.ShapeDtypeStruct((batch, n_tiles * ROW_TILE, D_MODEL), _F32),
        scratch_shapes=[pltpu.VMEM((ROW_TILE + MAX_WINDOW, E_POOL), _F32)],
        compiler_params=params,
        name="pool_layer",
    )(meta_pad, x, norm_w[0:1], pool_w_in[0].astype(_BF16), pool_w_grp[0].astype(_BF16), pool_scale[0:1],
      pool_w_out[0].astype(_BF16))

    out = pl.pallas_call(
        functools.partial(_hgrn_layer_kernel, 1),
        grid=grid,
        in_specs=[
            row_spec,
            _const_spec((1, D_MODEL)),
            _const_spec((D_MODEL, 4 * E_HGRN)),
            _const_spec((depth, E_HGRN)),
            _const_spec((1, E_HGRN)),
            _const_spec((E_HGRN, D_MODEL)),
            _const_spec((1, D_MODEL)),
        ],
        out_specs=x_spec,
        out_shape=jax.ShapeDtypeStruct((batch, seq, D_MODEL), _F32),
        scratch_shapes=[
            pltpu.VMEM((ROW_TILE, E_HGRN), _F32),
            pltpu.VMEM((ROW_TILE, E_HGRN), _F32),
            pltpu.VMEM((ROW_TILE, E_HGRN), _BF16),
            pltpu.VMEM((ROW_TILE, E_HGRN), _F32),
            pltpu.VMEM((ROW_TILE, E_HGRN), _F32),
            pltpu.VMEM((ROW_TILE, E_HGRN), _BF16),
            pltpu.VMEM((ROW_TILE, E_HGRN), _BF16),
            pltpu.VMEM((ROW_TILE, E_HGRN), _BF16),
            pltpu.VMEM((ROW_TILE, E_HGRN), _BF16),
            pltpu.VMEM((ROW_TILE, E_HGRN), _BF16),
            pltpu.VMEM((ROW_TILE, E_HGRN), _F32),
            pltpu.VMEM((N_HEADS, HEAD_DIM, HEAD_DIM), _F32),
        ],
        compiler_params=params,
        name="hgrn_layer",
    )(h1, norm_w[1:2], hgrn_w_in[0].astype(_BF16), hgrn_lb_logits, hgrn_o_norm[0:1], hgrn_w_out[0].astype(_BF16),
      final_norm_w.reshape(1, D_MODEL))
    return out
```

```python
import functools

import jax
import jax.numpy as jnp
from jax import lax
from jax.experimental import pallas as pl
from jax.experimental.pallas import tpu as pltpu

D_MODEL = 1024
N_META = 16
E_POOL = 2048
POOL_WINDOWS = (2, 4, 8, 16)
POOL_GROUP_DIM = E_POOL // len(POOL_WINDOWS)
MAX_WINDOW = max(POOL_WINDOWS)
E_HGRN = 2048
HEAD_DIM = 128
N_HEADS = E_HGRN // HEAD_DIM
EPS = 1e-6
LOG2_E = 1.4426950408889634

ROW_TILE = 256
PAD_ROWS = ROW_TILE - N_META
CHUNK = 128
HALF = CHUNK // 2
COL_TILE = 256
VMEM_LIMIT_BYTES = 60 * 1024 * 1024

_F32 = jnp.float32
_BF16 = jnp.bfloat16


def _rmsnorm(x, w):
    ms = jnp.mean(x * x, axis=-1, keepdims=True)
    return x * lax.rsqrt(ms + EPS) * w


def _silu(x):
    return x * jax.nn.sigmoid(x)


def _dot(a, b):
    return jnp.dot(a, b, preferred_element_type=_F32)


def _dot_nt(a, b):
    return lax.dot_general(a, b, (((1,), (1,)), ((), ())), preferred_element_type=_F32)


def _dot_tn(a, b):
    return lax.dot_general(a, b, (((0,), (0,)), ((), ())), preferred_element_type=_F32)


def _pool_layer_kernel(meta_ref, x_ref, nw_ref, win_ref, wgrp_ref, scale_ref, wout_ref, out_ref, vbuf):
    i = pl.program_id(1)

    @pl.when(i == 0)
    def _():
        vbuf[0:MAX_WINDOW, :] = jnp.zeros((MAX_WINDOW, E_POOL), _F32)

    h = jnp.where(i == 0, meta_ref[...], x_ref[0])
    hn = _rmsnorm(h, nw_ref[...]).astype(_BF16)
    v = _dot(hn, win_ref[:, 0:E_POOL])
    gate = _dot(hn, win_ref[:, E_POOL:2 * E_POOL])
    vbuf[MAX_WINDOW:MAX_WINDOW + ROW_TILE, :] = v

    pos = i * ROW_TILE - PAD_ROWS + lax.broadcasted_iota(jnp.int32, (ROW_TILE, 1), 0)

    parts = []
    for g, w in enumerate(POOL_WINDOWS):
        c0 = g * POOL_GROUP_DIM
        cols = slice(c0, c0 + POOL_GROUP_DIM)

        def rows(shift, cols=cols):
            return vbuf[MAX_WINDOW - shift:MAX_WINDOW - shift + ROW_TILE, cols]

        if w > 8:
            acc = rows(0) + rows(8)
            for j in range(1, 8):
                acc = acc + (rows(j) + rows(j + 8))
        else:
            acc = rows(0)
            for j in range(1, w):
                acc = acc + rows(j)
        cnt = jnp.clip(pos + 1, 1, w).astype(_F32)
        u = acc * (1.0 / cnt) - rows(0)
        parts.append(_dot(u.astype(_BF16), wgrp_ref[g]))
    u = jnp.concatenate(parts, axis=-1) * scale_ref[...]
    y = _dot((u * _silu(gate)).astype(_BF16), wout_ref[...])
    out_ref[0] = h + y

    vbuf[0:MAX_WINDOW, :] = vbuf[ROW_TILE:ROW_TILE + MAX_WINDOW, :]


def _hgrn_step(proj_out, rec_in, hp_ref, hr_ref, nw_ref, win_ref, onw_ref, wout_ref, fw_ref,
               out_ref, lb_s, ko_s, qin_s, kin_s, qst_s, kst_s, a_s, st_s):
    q_w, fp_w, v_w, gate_w = proj_out
    q_s, b_s, v_s, gate_s = rec_in
    chunks = [slice(c * CHUNK, (c + 1) * CHUNK) for c in range(ROW_TILE // CHUNK)]
    col_tiles = [slice(j * COL_TILE, (j + 1) * COL_TILE) for j in range(E_HGRN // COL_TILE)]
    heads_per_tile = COL_TILE // HEAD_DIM

    hn = _rmsnorm(hp_ref[0], nw_ref[...]).astype(_BF16)

    def project(part, dst, cs):
        dst[:, cs] = _dot(hn, win_ref[:, part * E_HGRN + cs.start:part * E_HGRN + cs.stop]).astype(dst.dtype)

    def forget_chain(cs):
        fp = b_s[:, cs]
        lb = lb_s[:, cs]
        t = jnp.exp2(jnp.abs(fp) * (-LOG2_E))
        r = 1.0 / (1.0 + t)
        tr = t * r
        sig_pos = jnp.where(fp >= 0, r, tr)
        sig_neg = jnp.where(fp >= 0, tr, r)
        b_s[:, cs] = jnp.log2(lb + (1.0 - lb) * sig_pos)
        ko_s[:, cs] = (1.0 - lb) * sig_neg

    row = lax.broadcasted_iota(jnp.int32, (CHUNK, CHUNK), 0)
    col = lax.broadcasted_iota(jnp.int32, (CHUNK, CHUNK), 1)
    causal = col <= row
    tri = causal.astype(_BF16)
    tri2 = jnp.concatenate([tri, tri], axis=1)

    def cumsum(rs, cs):
        g = b_s[rs, cs]
        g_hi = g.astype(_BF16)
        g_lo = (g - g_hi.astype(_F32)).astype(_BF16)
        b_s[rs, cs] = _dot(tri2, jnp.concatenate([g_hi, g_lo], axis=0))

    d_end = {}

    def decay_operands(c, j):
        rs, cs = chunks[c], col_tiles[j]
        b = b_s[rs, cs]
        b_mid = b_s[rs.start + HALF - 1:rs.start + HALF, cs]
        b_end = b_s[rs.stop - 1:rs.stop, cs]
        q = _silu(q_s[rs, cs])
        k = ko_s[rs, cs]
        qin_s[rs, cs] = (q * jnp.exp2(b - b_mid)).astype(_BF16)
        kin_s[rs, cs] = (k * jnp.exp2(b_mid - b)).astype(_BF16)
        qst_s[rs, cs] = (q * jnp.exp2(b)).astype(_BF16)
        kst_s[rs, cs] = (k * jnp.exp2(b_end - b)).astype(_BF16)
        d_end[c, j] = jnp.exp2(b_end)

    def scores(c, hd):
        rs, ls = chunks[c], slice(hd * HEAD_DIM, (hd + 1) * HEAD_DIM)
        a = _dot_nt(qin_s[rs, ls], kin_s[rs, ls])
        a_s[rs, ls] = jnp.where(causal, a, 0.0).astype(_BF16)

    def output_and_state(c, hd):
        rs, ls = chunks[c], slice(hd * HEAD_DIM, (hd + 1) * HEAD_DIM)
        st = st_s[hd]
        v = v_s[rs, ls]
        lhs = jnp.concatenate([qst_s[rs, ls], a_s[rs, ls]], axis=1)
        rhs = jnp.concatenate([st.astype(_BF16), v], axis=0)
        ko_s[rs, ls] = _dot(lhs, rhs)
        d_row = d_end[c, hd // heads_per_tile][:, (hd % heads_per_tile) * HEAD_DIM:][:, :HEAD_DIM]
        d_col = jnp.transpose(jnp.broadcast_to(d_row, (HEAD_DIM, HEAD_DIM)))
        st_s[hd] = st * d_col + _dot_tn(kst_s[rs, ls], v)

    gated = {}

    def norm_and_gate(hd):
        ls = slice(hd * HEAD_DIM, (hd + 1) * HEAD_DIM)
        o = ko_s[:, ls]
        on = o * lax.rsqrt(jnp.mean(o * o, axis=-1, keepdims=True) + EPS)
        gated[hd] = (on * onw_ref[:, ls] * _silu(gate_s[:, ls])).astype(_BF16)

    n_col = len(col_tiles)
    for j in range(n_col):
        project(0, q_w, col_tiles[j])
        forget_chain(col_tiles[j])
    for c in range(len(chunks)):
        for j in range(n_col):
            cumsum(chunks[c], col_tiles[j])
    for j in range(n_col):
        project(1, fp_w, col_tiles[j])
        decay_operands(0, j)
    for j in range(n_col):
        project(2, v_w, col_tiles[j])
        decay_operands(1, j)
    for c in range(len(chunks)):
        for hd in range(N_HEADS):
            scores(c, hd)
    for hd in range(N_HEADS):
        output_and_state(0, hd)
    for j in range(n_col):
        project(3, gate_w, col_tiles[j])
        for hd in range(j * heads_per_tile, (j + 1) * heads_per_tile):
            output_and_state(1, hd)
        if j > 0:
            for hd in range((j - 1) * heads_per_tile, j * heads_per_tile):
                norm_and_gate(hd)
    for hd in range((n_col - 1) * heads_per_tile, N_HEADS):
        norm_and_gate(hd)

    y = _dot(jnp.concatenate([gated[hd] for hd in range(N_HEADS)], axis=-1), wout_ref[...])
    out_ref[0] = _rmsnorm(hr_ref[0] + y, fw_ref[...])


def _hgrn_layer_kernel(layer, tiles_per_row, hp_ref, hr_ref, nw_ref, win_ref, lbl_ref, onw_ref, wout_ref, fw_ref,
                       out_ref, qa_s, fa_s, va_s, ga_s, qb_s, fb_s, vb_s, gb_s, lb_s, ko_s, qin_s, kin_s, qst_s,
                       kst_s, a_s, st_s):
    s = pl.program_id(0)
    set_a = (qa_s, fa_s, va_s, ga_s)
    set_b = (qb_s, fb_s, vb_s, gb_s)

    @pl.when(s == 0)
    def _():
        for ref in set_b:
            ref[...] = jnp.zeros(ref.shape, ref.dtype)
        logits = lbl_ref[...]
        e = jnp.exp(logits - jnp.max(logits, axis=0, keepdims=True))
        p = e / jnp.sum(e, axis=0, keepdims=True)
        lb_s[...] = jnp.sum(p[0:layer + 1], axis=0, keepdims=True) - p[0:1]

    @pl.when((s == 0) | (lax.rem(s - 1, tiles_per_row) == 0))
    def _():
        st_s[...] = jnp.zeros(st_s.shape, _F32)

    rest = (hp_ref, hr_ref, nw_ref, win_ref, onw_ref, wout_ref, fw_ref, out_ref,
            lb_s, ko_s, qin_s, kin_s, qst_s, kst_s, a_s, st_s)

    @pl.when(lax.rem(s, 2) == 0)
    def _():
        _hgrn_step(set_a, set_b, *rest)

    @pl.when(lax.rem(s, 2) == 1)
    def _():
        _hgrn_step(set_b, set_a, *rest)


def _const_spec(shape):
    zeros = (0,) * len(shape)
    return pl.BlockSpec(shape, lambda *_: zeros, pipeline_mode=pl.Buffered(1))


def kernel(x, meta_tokens, norm_w, pool_w_in, pool_w_grp, pool_scale, pool_w_out, hgrn_w_in, hgrn_lb_logits,
           hgrn_o_norm, hgrn_w_out, final_norm_w):
    batch, seq, d = x.shape
    depth = norm_w.shape[0]
    assert d == D_MODEL and seq % ROW_TILE == 0 and depth == 2
    assert meta_tokens.shape == (N_META, D_MODEL)
    n_tiles = seq // ROW_TILE + 1
    total_tiles = batch * n_tiles

    meta_pad = jnp.concatenate([jnp.zeros((PAD_ROWS, D_MODEL), x.dtype), meta_tokens.astype(x.dtype)], axis=0)
    row_spec = pl.BlockSpec((1, ROW_TILE, D_MODEL), lambda b, i: (b, i, 0))
    x_spec = pl.BlockSpec((1, ROW_TILE, D_MODEL), lambda b, i: (b, jnp.maximum(i - 1, 0), 0))

    h1 = pl.pallas_call(
        _pool_layer_kernel,
        grid=(batch, n_tiles),
        in_specs=[
            _const_spec((ROW_TILE, D_MODEL)),
            x_spec,
            _const_spec((1, D_MODEL)),
            _const_spec((D_MODEL, 2 * E_POOL)),
            _const_spec((len(POOL_WINDOWS), POOL_GROUP_DIM, POOL_GROUP_DIM)),
            _const_spec((1, E_POOL)),
            _const_spec((E_POOL, D_MODEL)),
        ],
        out_specs=row_spec,
        out_shape=jax.ShapeDtypeStruct((batch, n_tiles * ROW_TILE, D_MODEL), _F32),
        scratch_shapes=[pltpu.VMEM((ROW_TILE + MAX_WINDOW, E_POOL), _F32)],
        compiler_params=pltpu.CompilerParams(dimension_semantics=("arbitrary", "arbitrary"),
                                             vmem_limit_bytes=VMEM_LIMIT_BYTES),
        name="pool_layer",
    )(meta_pad, x, norm_w[0:1], pool_w_in[0].astype(_BF16), pool_w_grp[0].astype(_BF16), pool_scale[0:1],
      pool_w_out[0].astype(_BF16))

    def proj_tile(s):
        t = jnp.minimum(s, total_tiles - 1)
        return (t // n_tiles, t % n_tiles, 0)

    def rec_tile(s):
        t = jnp.maximum(s - 1, 0)
        return (t // n_tiles, t % n_tiles, 0)

    def out_tile(s):
        t = jnp.maximum(s - 1, 0)
        return (t // n_tiles, jnp.maximum(t % n_tiles - 1, 0), 0)

    tile_f32 = pltpu.VMEM((ROW_TILE, E_HGRN), _F32)
    tile_bf16 = pltpu.VMEM((ROW_TILE, E_HGRN), _BF16)
    out = pl.pallas_call(
        functools.partial(_hgrn_layer_kernel, 1, n_tiles),
        grid=(total_tiles + 1,),
        in_specs=[
            pl.BlockSpec((1, ROW_TILE, D_MODEL), proj_tile),
            pl.BlockSpec((1, ROW_TILE, D_MODEL), rec_tile),
            _const_spec((1, D_MODEL)),
            _const_spec((D_MODEL, 4 * E_HGRN)),
            _const_spec((depth, E_HGRN)),
            _const_spec((1, E_HGRN)),
            _const_spec((E_HGRN, D_MODEL)),
            _const_spec((1, D_MODEL)),
        ],
        out_specs=pl.BlockSpec((1, ROW_TILE, D_MODEL), out_tile),
        out_shape=jax.ShapeDtypeStruct((batch, seq, D_MODEL), _F32),
        scratch_shapes=[
            tile_f32, tile_f32, tile_bf16, tile_f32,
            tile_f32, tile_f32, tile_bf16, tile_f32,
            pltpu.VMEM((1, E_HGRN), _F32),
            tile_f32,
            tile_bf16,
            tile_bf16,
            tile_bf16,
            tile_bf16,
            tile_bf16,
            pltpu.VMEM((N_HEADS, HEAD_DIM, HEAD_DIM), _F32),
        ],
        compiler_params=pltpu.CompilerParams(dimension_semantics=("arbitrary",),
                                             vmem_limit_bytes=VMEM_LIMIT_BYTES),
        name="hgrn_layer",
    )(h1, h1, norm_w[1:2], hgrn_w_in[0].astype(_BF16), hgrn_lb_logits, hgrn_o_norm[0:1],
      hgrn_w_out[0].astype(_BF16), final_norm_w.reshape(1, D_MODEL))
    return out
```

```python
import functools

import jax
import jax.numpy as jnp
from jax import lax
from jax.experimental import pallas as pl
from jax.experimental.pallas import tpu as pltpu

D_MODEL = 1024
N_META = 16
E_POOL = 2048
POOL_WINDOWS = (2, 4, 8, 16)
POOL_GROUP_DIM = E_POOL // len(POOL_WINDOWS)
MAX_WINDOW = max(POOL_WINDOWS)
E_HGRN = 2048
HEAD_DIM = 128
N_HEADS = E_HGRN // HEAD_DIM
EPS = 1e-6

ROW_TILE = 256
PAD_ROWS = ROW_TILE - N_META
CHUNK = 128
HALF = CHUNK // 2
VMEM_LIMIT_BYTES = 56 * 1024 * 1024

_F32 = jnp.float32
_BF16 = jnp.bfloat16


def _rmsnorm(x, w):
    ms = jnp.mean(x * x, axis=-1, keepdims=True)
    return x * lax.rsqrt(ms + EPS) * w


def _silu(x):
    return x * jax.nn.sigmoid(x)


def _dot(a, b):
    return jnp.dot(a, b, preferred_element_type=_F32)


def _dot_nt(a, b):
    return lax.dot_general(a, b, (((1,), (1,)), ((), ())), preferred_element_type=_F32)


def _dot_tn(a, b):
    return lax.dot_general(a, b, (((0,), (0,)), ((), ())), preferred_element_type=_F32)


def _pool_layer_kernel(meta_ref, x_ref, nw_ref, win_ref, wgrp_ref, scale_ref, wout_ref, out_ref, vbuf, meta_carry):
    row = pl.program_id(0)
    i = pl.program_id(1)
    reuse_meta = (i == 0) & (row > 0)

    @pl.when(reuse_meta)
    def _():
        vbuf[0:MAX_WINDOW, :] = meta_carry[...]
        out_ref[0] = jnp.zeros((ROW_TILE, D_MODEL), _F32)

    @pl.when(jnp.logical_not(reuse_meta))
    def _():
        @pl.when(i == 0)
        def _():
            vbuf[0:MAX_WINDOW, :] = jnp.zeros((MAX_WINDOW, E_POOL), _F32)

        h = jnp.where(i == 0, meta_ref[...], x_ref[0])
        hn = _rmsnorm(h, nw_ref[...]).astype(_BF16)
        v = _dot(hn, win_ref[:, 0:E_POOL])
        gate = _dot(hn, win_ref[:, E_POOL:2 * E_POOL])
        vbuf[MAX_WINDOW:MAX_WINDOW + ROW_TILE, :] = v

        pos = i * ROW_TILE - PAD_ROWS + lax.broadcasted_iota(jnp.int32, (ROW_TILE, 1), 0)

        parts = []
        for g, w in enumerate(POOL_WINDOWS):
            c0 = g * POOL_GROUP_DIM
            cols = slice(c0, c0 + POOL_GROUP_DIM)

            def rows(shift, cols=cols):
                return vbuf[MAX_WINDOW - shift:MAX_WINDOW - shift + ROW_TILE, cols]

            if w > 8:
                acc = rows(0) + rows(8)
                for j in range(1, 8):
                    acc = acc + (rows(j) + rows(j + 8))
            else:
                acc = rows(0)
                for j in range(1, w):
                    acc = acc + rows(j)
            cnt = jnp.clip(pos + 1, 1, w).astype(_F32)
            u = acc * (1.0 / cnt) - rows(0)
            parts.append(_dot(u.astype(_BF16), wgrp_ref[g]))
        u = jnp.concatenate(parts, axis=-1) * scale_ref[...]
        y = _dot((u * _silu(gate)).astype(_BF16), wout_ref[...])
        out_ref[0] = h + y

        vbuf[0:MAX_WINDOW, :] = vbuf[ROW_TILE:ROW_TILE + MAX_WINDOW, :]

        @pl.when(i == 0)
        def _():
            meta_carry[...] = vbuf[0:MAX_WINDOW, :]


def _hgrn_layer_kernel(layer, h_ref, nw_ref, win_ref, lbl_ref, onw_ref, wout_ref, fw_ref, out_ref,
                       q_s, k_s, v_s, g_s, gate_s, qin_s, kin_s, qst_s, kst_s, a_s, o_s, st_s, meta_state):
    row = pl.program_id(0)
    i = pl.program_id(1)
    reuse_meta = (i == 0) & (row > 0)

    @pl.when(reuse_meta)
    def _():
        st_s[...] = meta_state[...]

    @pl.when(jnp.logical_not(reuse_meta))
    def _():
        @pl.when(i == 0)
        def _():
            st_s[...] = jnp.zeros(st_s.shape, _F32)

        _hgrn_tile(layer, h_ref, nw_ref, win_ref, lbl_ref, onw_ref, wout_ref, fw_ref, out_ref,
                   q_s, k_s, v_s, g_s, gate_s, qin_s, kin_s, qst_s, kst_s, a_s, o_s, st_s)

        @pl.when(i == 0)
        def _():
            meta_state[...] = st_s[...]


def _hgrn_tile(layer, h_ref, nw_ref, win_ref, lbl_ref, onw_ref, wout_ref, fw_ref, out_ref,
               q_s, k_s, v_s, g_s, gate_s, qin_s, kin_s, qst_s, kst_s, a_s, o_s, st_s):
    logits = lbl_ref[...]
    e = jnp.exp(logits - jnp.max(logits, axis=0, keepdims=True))
    p = e / jnp.sum(e, axis=0, keepdims=True)
    lb = jnp.sum(p[0:layer + 1], axis=0, keepdims=True) - p[0:1]

    h = h_ref[0]
    hn = _rmsnorm(h, nw_ref[...]).astype(_BF16)
    q_s[...] = _silu(_dot(hn, win_ref[:, 0:E_HGRN]))
    fp = _dot(hn, win_ref[:, E_HGRN:2 * E_HGRN])
    t = jnp.exp(-jnp.abs(fp))
    r = 1.0 / (1.0 + t)
    tr = t * r
    sig_pos = jnp.where(fp >= 0, r, tr)
    sig_neg = jnp.where(fp >= 0, tr, r)
    g_s[...] = jnp.log(lb + (1.0 - lb) * sig_pos)
    k_s[...] = (1.0 - lb) * sig_neg
    v_s[...] = _dot(hn, win_ref[:, 2 * E_HGRN:3 * E_HGRN]).astype(_BF16)
    gate_s[...] = _silu(_dot(hn, win_ref[:, 3 * E_HGRN:4 * E_HGRN]))

    row = lax.broadcasted_iota(jnp.int32, (CHUNK, CHUNK), 0)
    col = lax.broadcasted_iota(jnp.int32, (CHUNK, CHUNK), 1)
    causal = col <= row
    tri = causal.astype(_BF16)
    tri2 = jnp.concatenate([tri, tri], axis=1)

    d_end = []
    for c in range(ROW_TILE // CHUNK):
        rs = slice(c * CHUNK, (c + 1) * CHUNK)
        g = g_s[rs, :]
        g_hi = g.astype(_BF16)
        g_lo = (g - g_hi.astype(_F32)).astype(_BF16)
        b = _dot(tri2, jnp.concatenate([g_hi, g_lo], axis=0))
        b_mid = b[HALF - 1:HALF, :]
        b_end = b[CHUNK - 1:CHUNK, :]
        q = q_s[rs, :]
        k = k_s[rs, :]
        qin_s[rs, :] = (q * jnp.exp(b - b_mid)).astype(_BF16)
        kin_s[rs, :] = (k * jnp.exp(b_mid - b)).astype(_BF16)
        qst_s[rs, :] = (q * jnp.exp(b)).astype(_BF16)
        kst_s[rs, :] = (k * jnp.exp(b_end - b)).astype(_BF16)
        d_end.append(jnp.exp(b_end))

    for c in range(ROW_TILE // CHUNK):
        rs = slice(c * CHUNK, (c + 1) * CHUNK)
        for hd in range(N_HEADS):
            ls = slice(hd * HEAD_DIM, (hd + 1) * HEAD_DIM)
            a = _dot_nt(qin_s[rs, ls], kin_s[rs, ls])
            a_s[rs, ls] = jnp.where(causal, a, 0.0).astype(_BF16)

    for c in range(ROW_TILE // CHUNK):
        rs = slice(c * CHUNK, (c + 1) * CHUNK)
        for hd in range(N_HEADS):
            ls = slice(hd * HEAD_DIM, (hd + 1) * HEAD_DIM)
            st = st_s[hd]
            v = v_s[rs, ls]
            lhs = jnp.concatenate([qst_s[rs, ls], a_s[rs, ls]], axis=1)
            rhs = jnp.concatenate([st.astype(_BF16), v], axis=0)
            o_s[rs, ls] = _dot(lhs, rhs)
            d_col = jnp.transpose(jnp.broadcast_to(d_end[c][:, ls], (HEAD_DIM, HEAD_DIM)))
            st_s[hd] = st * d_col + _dot_tn(kst_s[rs, ls], v)

    parts = []
    for hd in range(N_HEADS):
        ls = slice(hd * HEAD_DIM, (hd + 1) * HEAD_DIM)
        o = o_s[:, ls]
        parts.append(o * lax.rsqrt(jnp.mean(o * o, axis=-1, keepdims=True) + EPS))
    o = jnp.concatenate(parts, axis=-1) * onw_ref[...]
    y = _dot((o * gate_s[...]).astype(_BF16), wout_ref[...])
    out_ref[0] = _rmsnorm(h + y, fw_ref[...])


def _const_spec(shape):
    zeros = (0,) * len(shape)
    return pl.BlockSpec(shape, lambda b, i: zeros, pipeline_mode=pl.Buffered(1))


def kernel(x, meta_tokens, norm_w, pool_w_in, pool_w_grp, pool_scale, pool_w_out, hgrn_w_in, hgrn_lb_logits,
           hgrn_o_norm, hgrn_w_out, final_norm_w):
    batch, seq, d = x.shape
    depth = norm_w.shape[0]
    assert d == D_MODEL and seq % ROW_TILE == 0 and depth == 2
    assert meta_tokens.shape == (N_META, D_MODEL)
    n_tiles = seq // ROW_TILE + 1
    grid = (batch, n_tiles)
    params = pltpu.CompilerParams(dimension_semantics=("arbitrary", "arbitrary"),
                                  vmem_limit_bytes=VMEM_LIMIT_BYTES)

    meta_pad = jnp.concatenate([jnp.zeros((PAD_ROWS, D_MODEL), x.dtype), meta_tokens.astype(x.dtype)], axis=0)
    row_spec = pl.BlockSpec((1, ROW_TILE, D_MODEL), lambda b, i: (b, i, 0))
    x_spec = pl.BlockSpec((1, ROW_TILE, D_MODEL), lambda b, i: (b, jnp.maximum(i - 1, 0), 0))

    h1 = pl.pallas_call(
        _pool_layer_kernel,
        grid=grid,
        in_specs=[
            _const_spec((ROW_TILE, D_MODEL)),
            x_spec,
            _const_spec((1, D_MODEL)),
            _const_spec((D_MODEL, 2 * E_POOL)),
            _const_spec((len(POOL_WINDOWS), POOL_GROUP_DIM, POOL_GROUP_DIM)),
            _const_spec((1, E_POOL)),
            _const_spec((E_POOL, D_MODEL)),
        ],
        out_specs=row_spec,
        out_shape=jax.ShapeDtypeStruct((batch, n_tiles * ROW_TILE, D_MODEL), _F32),
        scratch_shapes=[
            pltpu.VMEM((ROW_TILE + MAX_WINDOW, E_POOL), _F32),
            pltpu.VMEM((MAX_WINDOW, E_POOL), _F32),
        ],
        compiler_params=params,
        name="pool_layer",
    )(meta_pad, x, norm_w[0:1], pool_w_in[0].astype(_BF16), pool_w_grp[0].astype(_BF16), pool_scale[0:1],
      pool_w_out[0].astype(_BF16))

    out = pl.pallas_call(
        functools.partial(_hgrn_layer_kernel, 1),
        grid=grid,
        in_specs=[
            row_spec,
            _const_spec((1, D_MODEL)),
            _const_spec((D_MODEL, 4 * E_HGRN)),
            _const_spec((depth, E_HGRN)),
            _const_spec((1, E_HGRN)),
            _const_spec((E_HGRN, D_MODEL)),
            _const_spec((1, D_MODEL)),
        ],
        out_specs=x_spec,
        out_shape=jax.ShapeDtypeStruct((batch, seq, D_MODEL), _F32),
        scratch_shapes=[
            pltpu.VMEM((ROW_TILE, E_HGRN), _F32),
            pltpu.VMEM((ROW_TILE, E_HGRN), _F32),
            pltpu.VMEM((ROW_TILE, E_HGRN), _BF16),
            pltpu.VMEM((ROW_TILE, E_HGRN), _F32),
            pltpu.VMEM((ROW_TILE, E_HGRN), _F32),
            pltpu.VMEM((ROW_TILE, E_HGRN), _BF16),
            pltpu.VMEM((ROW_TILE, E_HGRN), _BF16),
            pltpu.VMEM((ROW_TILE, E_HGRN), _BF16),
            pltpu.VMEM((ROW_TILE, E_HGRN), _BF16),
            pltpu.VMEM((ROW_TILE, E_HGRN), _BF16),
            pltpu.VMEM((ROW_TILE, E_HGRN), _F32),
            pltpu.VMEM((N_HEADS, HEAD_DIM, HEAD_DIM), _F32),
            pltpu.VMEM((N_HEADS, HEAD_DIM, HEAD_DIM), _F32),
        ],
        compiler_params=params,
        name="hgrn_layer",
    )(h1, norm_w[1:2], hgrn_w_in[0].astype(_BF16), hgrn_lb_logits, hgrn_o_norm[0:1], hgrn_w_out[0].astype(_BF16),
      final_norm_w.reshape(1, D_MODEL))
    return out
```

```python
import functools

import jax
import jax.numpy as jnp
from jax import lax
from jax.experimental import pallas as pl
from jax.experimental.pallas import tpu as pltpu

D_MODEL = 1024
N_META = 16
E_POOL = 2048
POOL_WINDOWS = (2, 4, 8, 16)
POOL_GROUP_DIM = E_POOL // len(POOL_WINDOWS)
MAX_WINDOW = max(POOL_WINDOWS)
E_HGRN = 2048
HEAD_DIM = 128
N_HEADS = E_HGRN // HEAD_DIM
EPS = 1e-6
LOG2_E = 1.4426950408889634

POOL_TILE = 512
POOL_PAD = POOL_TILE - N_META
ROW_TILE = 256
CHUNK = 128
HALF = CHUNK // 2
VMEM_LIMIT_BYTES = 56 * 1024 * 1024

_F32 = jnp.float32
_BF16 = jnp.bfloat16


def _rmsnorm(x, w):
    ms = jnp.mean(x * x, axis=-1, keepdims=True)
    return x * lax.rsqrt(ms + EPS) * w


def _silu(x):
    return x * jax.nn.sigmoid(x)


def _dot(a, b):
    return jnp.dot(a, b, preferred_element_type=_F32)


def _dot_nt(a, b):
    return lax.dot_general(a, b, (((1,), (1,)), ((), ())), preferred_element_type=_F32)


def _dot_tn(a, b):
    return lax.dot_general(a, b, (((0,), (0,)), ((), ())), preferred_element_type=_F32)


def _pool_layer_kernel(meta_ref, x_ref, nw_ref, win_ref, wgrp_ref, scale_ref, wout_ref, out_ref, vbuf, meta_carry):
    row = pl.program_id(0)
    i = pl.program_id(1)
    reuse_meta = (i == 0) & (row > 0)

    @pl.when(reuse_meta)
    def _():
        vbuf[0:MAX_WINDOW, :] = meta_carry[...]
        out_ref[0] = jnp.zeros((POOL_TILE, D_MODEL), _F32)

    @pl.when(jnp.logical_not(reuse_meta))
    def _():
        @pl.when(i == 0)
        def _():
            vbuf[0:MAX_WINDOW, :] = jnp.zeros((MAX_WINDOW, E_POOL), _F32)

        h = jnp.where(i == 0, meta_ref[...], x_ref[0])
        hn = _rmsnorm(h, nw_ref[...]).astype(_BF16)
        v = _dot(hn, win_ref[:, 0:E_POOL])
        gate = _dot(hn, win_ref[:, E_POOL:2 * E_POOL])
        vbuf[MAX_WINDOW:MAX_WINDOW + POOL_TILE, :] = v

        pos = i * POOL_TILE - POOL_PAD + lax.broadcasted_iota(jnp.int32, (POOL_TILE, 1), 0)

        parts = []
        for g, w in enumerate(POOL_WINDOWS):
            c0 = g * POOL_GROUP_DIM
            cols = slice(c0, c0 + POOL_GROUP_DIM)

            def rows(shift, cols=cols):
                return vbuf[MAX_WINDOW - shift:MAX_WINDOW - shift + POOL_TILE, cols]

            if w > 8:
                acc = rows(0) + rows(8)
                for j in range(1, 8):
                    acc = acc + (rows(j) + rows(j + 8))
            else:
                acc = rows(0)
                for j in range(1, w):
                    acc = acc + rows(j)
            cnt = jnp.clip(pos + 1, 1, w).astype(_F32)
            u = acc * (1.0 / cnt) - rows(0)
            parts.append(_dot(u.astype(_BF16), wgrp_ref[g]))
        u = jnp.concatenate(parts, axis=-1) * scale_ref[...]
        y = _dot((u * _silu(gate)).astype(_BF16), wout_ref[...])
        out_ref[0] = h + y

        vbuf[0:MAX_WINDOW, :] = vbuf[POOL_TILE:POOL_TILE + MAX_WINDOW, :]

        @pl.when(i == 0)
        def _():
            meta_carry[...] = vbuf[0:MAX_WINDOW, :]


def _hgrn_layer_kernel(layer, h_ref, nw_ref, win_ref, lbl_ref, onw_ref, wout_ref, fw_ref, out_ref,
                       q_s, k_s, v_s, g_s, gate_s, qin_s, kin_s, qst_s, kst_s, a_s, o_s, st_s, meta_state):
    row = pl.program_id(0)
    i = pl.program_id(1)
    reuse_meta = (i == 0) & (row > 0)

    @pl.when(reuse_meta)
    def _():
        st_s[...] = meta_state[...]

    @pl.when(jnp.logical_not(reuse_meta))
    def _():
        @pl.when(i == 0)
        def _():
            st_s[...] = jnp.zeros(st_s.shape, _F32)

        _hgrn_tile(layer, h_ref, nw_ref, win_ref, lbl_ref, onw_ref, wout_ref, fw_ref, out_ref,
                   q_s, k_s, v_s, g_s, gate_s, qin_s, kin_s, qst_s, kst_s, a_s, o_s, st_s)

        @pl.when(i == 0)
        def _():
            meta_state[...] = st_s[...]


def _hgrn_tile(layer, h_ref, nw_ref, win_ref, lbl_ref, onw_ref, wout_ref, fw_ref, out_ref,
               q_s, k_s, v_s, g_s, gate_s, qin_s, kin_s, qst_s, kst_s, a_s, o_s, st_s):
    logits = lbl_ref[...]
    e = jnp.exp(logits - jnp.max(logits, axis=0, keepdims=True))
    p = e / jnp.sum(e, axis=0, keepdims=True)
    lb = jnp.sum(p[0:layer + 1], axis=0, keepdims=True) - p[0:1]

    h = h_ref[0]
    hn = _rmsnorm(h, nw_ref[...]).astype(_BF16)
    q_s[...] = _silu(_dot(hn, win_ref[:, 0:E_HGRN]))
    fp = _dot(hn, win_ref[:, E_HGRN:2 * E_HGRN])
    t = jnp.exp2(jnp.abs(fp) * (-LOG2_E))
    r = 1.0 / (1.0 + t)
    tr = t * r
    sig_pos = jnp.where(fp >= 0, r, tr)
    sig_neg = jnp.where(fp >= 0, tr, r)
    g_s[...] = jnp.log2(lb + (1.0 - lb) * sig_pos)
    k_s[...] = (1.0 - lb) * sig_neg
    v_s[...] = _dot(hn, win_ref[:, 2 * E_HGRN:3 * E_HGRN]).astype(_BF16)
    gate_s[...] = _silu(_dot(hn, win_ref[:, 3 * E_HGRN:4 * E_HGRN]))

    row = lax.broadcasted_iota(jnp.int32, (CHUNK, CHUNK), 0)
    col = lax.broadcasted_iota(jnp.int32, (CHUNK, CHUNK), 1)
    causal = col <= row
    tri = causal.astype(_BF16)
    tri2 = jnp.concatenate([tri, tri], axis=1)

    d_end = []
    for c in range(ROW_TILE // CHUNK):
        rs = slice(c * CHUNK, (c + 1) * CHUNK)
        g = g_s[rs, :]
        g_hi = g.astype(_BF16)
        g_lo = (g - g_hi.astype(_F32)).astype(_BF16)
        b = _dot(tri2, jnp.concatenate([g_hi, g_lo], axis=0))
        b_mid = b[HALF - 1:HALF, :]
        b_end = b[CHUNK - 1:CHUNK, :]
        q = q_s[rs, :]
        k = k_s[rs, :]
        qin_s[rs, :] = (q * jnp.exp2(b - b_mid)).astype(_BF16)
        kin_s[rs, :] = (k * jnp.exp2(b_mid - b)).astype(_BF16)
        qst_s[rs, :] = (q * jnp.exp2(b)).astype(_BF16)
        kst_s[rs, :] = (k * jnp.exp2(b_end - b)).astype(_BF16)
        d_end.append(jnp.exp2(b_end))

    for c in range(ROW_TILE // CHUNK):
        rs = slice(c * CHUNK, (c + 1) * CHUNK)
        for hd in range(N_HEADS):
            ls = slice(hd * HEAD_DIM, (hd + 1) * HEAD_DIM)
            a = _dot_nt(qin_s[rs, ls], kin_s[rs, ls])
            a_s[rs, ls] = jnp.where(causal, a, 0.0).astype(_BF16)

    for c in range(ROW_TILE // CHUNK):
        rs = slice(c * CHUNK, (c + 1) * CHUNK)
        for hd in range(N_HEADS):
            ls = slice(hd * HEAD_DIM, (hd + 1) * HEAD_DIM)
            st = st_s[hd]
            v = v_s[rs, ls]
            lhs = jnp.concatenate([qst_s[rs, ls], a_s[rs, ls]], axis=1)
            rhs = jnp.concatenate([st.astype(_BF16), v], axis=0)
            o_s[rs, ls] = _dot(lhs, rhs)
            d_col = jnp.transpose(jnp.broadcast_to(d_end[c][:, ls], (HEAD_DIM, HEAD_DIM)))
            st_s[hd] = st * d_col + _dot_tn(kst_s[rs, ls], v)

    parts = []
    for hd in range(N_HEADS):
        ls = slice(hd * HEAD_DIM, (hd + 1) * HEAD_DIM)
        o = o_s[:, ls]
        parts.append(o * lax.rsqrt(jnp.mean(o * o, axis=-1, keepdims=True) + EPS))
    o = jnp.concatenate(parts, axis=-1) * onw_ref[...]
    y = _dot((o * gate_s[...]).astype(_BF16), wout_ref[...])
    out_ref[0] = _rmsnorm(h + y, fw_ref[...])


def _const_spec(shape):
    zeros = (0,) * len(shape)
    return pl.BlockSpec(shape, lambda b, i: zeros, pipeline_mode=pl.Buffered(1))


def kernel(x, meta_tokens, norm_w, pool_w_in, pool_w_grp, pool_scale, pool_w_out, hgrn_w_in, hgrn_lb_logits,
           hgrn_o_norm, hgrn_w_out, final_norm_w):
    batch, seq, d = x.shape
    depth = norm_w.shape[0]
    assert d == D_MODEL and seq % POOL_TILE == 0 and POOL_TILE % ROW_TILE == 0 and depth == 2
    assert meta_tokens.shape == (N_META, D_MODEL)
    params = pltpu.CompilerParams(dimension_semantics=("arbitrary", "arbitrary"),
                                  vmem_limit_bytes=VMEM_LIMIT_BYTES)

    meta_pad = jnp.concatenate([jnp.zeros((POOL_PAD, D_MODEL), x.dtype), meta_tokens.astype(x.dtype)], axis=0)
    pool_tiles = seq // POOL_TILE + 1

    def tile_spec(rows, first):
        return pl.BlockSpec((1, rows, D_MODEL), lambda b, i: (b, jnp.maximum(i + first, 0), 0))

    h1 = pl.pallas_call(
        _pool_layer_kernel,
        grid=(batch, pool_tiles),
        in_specs=[
            _const_spec((POOL_TILE, D_MODEL)),
            tile_spec(POOL_TILE, -1),
            _const_spec((1, D_MODEL)),
            _const_spec((D_MODEL, 2 * E_POOL)),
            _const_spec((len(POOL_WINDOWS), POOL_GROUP_DIM, POOL_GROUP_DIM)),
            _const_spec((1, E_POOL)),
            _const_spec((E_POOL, D_MODEL)),
        ],
        out_specs=tile_spec(POOL_TILE, 0),
        out_shape=jax.ShapeDtypeStruct((batch, pool_tiles * POOL_TILE, D_MODEL), _F32),
        scratch_shapes=[
            pltpu.VMEM((POOL_TILE + MAX_WINDOW, E_POOL), _F32),
            pltpu.VMEM((MAX_WINDOW, E_POOL), _F32),
        ],
        compiler_params=params,
        name="pool_layer",
    )(meta_pad, x, norm_w[0:1], pool_w_in[0].astype(_BF16), pool_w_grp[0].astype(_BF16), pool_scale[0:1],
      pool_w_out[0].astype(_BF16))

    out = pl.pallas_call(
        functools.partial(_hgrn_layer_kernel, 1),
        grid=(batch, seq // ROW_TILE + 1),
        in_specs=[
            tile_spec(ROW_TILE, POOL_TILE // ROW_TILE - 1),
            _const_spec((1, D_MODEL)),
            _const_spec((D_MODEL, 4 * E_HGRN)),
            _const_spec((depth, E_HGRN)),
            _const_spec((1, E_HGRN)),
            _const_spec((E_HGRN, D_MODEL)),
            _const_spec((1, D_MODEL)),
        ],
        out_specs=tile_spec(ROW_TILE, -1),
        out_shape=jax.ShapeDtypeStruct((batch, seq, D_MODEL), _F32),
        scratch_shapes=[
            pltpu.VMEM((ROW_TILE, E_HGRN), _F32),
            pltpu.VMEM((ROW_TILE, E_HGRN), _F32),
            pltpu.VMEM((ROW_TILE, E_HGRN), _BF16),
            pltpu.VMEM((ROW_TILE, E_HGRN), _F32),
            pltpu.VMEM((ROW_TILE, E_HGRN), _F32),
            pltpu.VMEM((ROW_TILE, E_HGRN), _BF16),
            pltpu.VMEM((ROW_TILE, E_HGRN), _BF16),
            pltpu.VMEM((ROW_TILE, E_HGRN), _BF16),
            pltpu.VMEM((ROW_TILE, E_HGRN), _BF16),
            pltpu.VMEM((ROW_TILE, E_HGRN), _BF16),
            pltpu.VMEM((ROW_TILE, E_HGRN), _F32),
            pltpu.VMEM((N_HEADS, HEAD_DIM, HEAD_DIM), _F32),
            pltpu.VMEM((N_HEADS, HEAD_DIM, HEAD_DIM), _F32),
        ],
        compiler_params=params,
        name="hgrn_layer",
    )(h1, norm_w[1:2], hgrn_w_in[0].astype(_BF16), hgrn_lb_logits, hgrn_o_norm[0:1], hgrn_w_out[0].astype(_BF16),
      final_norm_w.reshape(1, D_MODEL))
    return out
```

```python
import functools

import jax
import jax.numpy as jnp
from jax import lax
from jax.experimental import pallas as pl
from jax.experimental.pallas import tpu as pltpu

D_MODEL = 1024
N_META = 16
E_POOL = 2048
POOL_WINDOWS = (2, 4, 8, 16)
POOL_GROUP_DIM = E_POOL // len(POOL_WINDOWS)
MAX_WINDOW = max(POOL_WINDOWS)
E_HGRN = 2048
HEAD_DIM = 128
N_HEADS = E_HGRN // HEAD_DIM
EPS = 1e-6
LOG2_E = 1.4426950408889634

POOL_TILE = 512
POOL_PAD = POOL_TILE - N_META
ROW_TILE = 256
CHUNK = 128
HALF = CHUNK // 2
VMEM_LIMIT_BYTES = 56 * 1024 * 1024

_F32 = jnp.float32
_BF16 = jnp.bfloat16


def _rmsnorm(x, w):
    ms = jnp.mean(x * x, axis=-1, keepdims=True)
    return x * lax.rsqrt(ms + EPS) * w


def _silu(x):
    return x * jax.nn.sigmoid(x)


def _dot(a, b):
    return jnp.dot(a, b, preferred_element_type=_F32)


def _dot_nt(a, b):
    return lax.dot_general(a, b, (((1,), (1,)), ((), ())), preferred_element_type=_F32)


def _dot_tn(a, b):
    return lax.dot_general(a, b, (((0,), (0,)), ((), ())), preferred_element_type=_F32)


def _pool_layer_kernel(meta_ref, x_ref, nw_ref, win_ref, wgrp_ref, scale_ref, wout_ref, out_ref, vbuf, meta_carry):
    row = pl.program_id(0)
    i = pl.program_id(1)
    reuse_meta = (i == 0) & (row > 0)

    @pl.when(reuse_meta)
    def _():
        vbuf[0:MAX_WINDOW, :] = meta_carry[...]
        out_ref[0] = jnp.zeros((POOL_TILE, D_MODEL), _F32)

    @pl.when(jnp.logical_not(reuse_meta))
    def _():
        @pl.when(i == 0)
        def _():
            vbuf[0:MAX_WINDOW, :] = jnp.zeros((MAX_WINDOW, E_POOL), _F32)

        h = jnp.where(i == 0, meta_ref[...], x_ref[0])
        hn = _rmsnorm(h, nw_ref[...]).astype(_BF16)
        v = _dot(hn, win_ref[:, 0:E_POOL])
        gate = _dot(hn, win_ref[:, E_POOL:2 * E_POOL])
        vbuf[MAX_WINDOW:MAX_WINDOW + POOL_TILE, :] = v

        pos = i * POOL_TILE - POOL_PAD + lax.broadcasted_iota(jnp.int32, (POOL_TILE, 1), 0)

        parts = []
        for g, w in enumerate(POOL_WINDOWS):
            c0 = g * POOL_GROUP_DIM
            cols = slice(c0, c0 + POOL_GROUP_DIM)

            def rows(shift, cols=cols):
                return vbuf[MAX_WINDOW - shift:MAX_WINDOW - shift + POOL_TILE, cols]

            if w > 8:
                acc = rows(0) + rows(8)
                for j in range(1, 8):
                    acc = acc + (rows(j) + rows(j + 8))
            else:
                acc = rows(0)
                for j in range(1, w):
                    acc = acc + rows(j)
            cnt = jnp.clip(pos + 1, 1, w).astype(_F32)
            u = acc * (1.0 / cnt) - rows(0)
            parts.append(_dot(u.astype(_BF16), wgrp_ref[g]))
        u = jnp.concatenate(parts, axis=-1) * scale_ref[...]
        y = _dot((u * _silu(gate)).astype(_BF16), wout_ref[...])
        out_ref[0] = h + y

        vbuf[0:MAX_WINDOW, :] = vbuf[POOL_TILE:POOL_TILE + MAX_WINDOW, :]

        @pl.when(i == 0)
        def _():
            meta_carry[...] = vbuf[0:MAX_WINDOW, :]


def _hgrn_layer_kernel(layer, h_ref, nw_ref, win_ref, lbl_ref, onw_ref, wout_ref, fw_ref, out_ref,
                       q_s, k_s, v_s, g_s, gate_s, qin_s, kin_s, a_s, o_s, st_s, meta_state):
    row = pl.program_id(0)
    i = pl.program_id(1)
    reuse_meta = (i == 0) & (row > 0)

    @pl.when(reuse_meta)
    def _():
        st_s[...] = meta_state[...]

    @pl.when(jnp.logical_not(reuse_meta))
    def _():
        @pl.when(i == 0)
        def _():
            st_s[...] = jnp.zeros(st_s.shape, _F32)

        _hgrn_tile(layer, h_ref, nw_ref, win_ref, lbl_ref, onw_ref, wout_ref, fw_ref, out_ref,
                   q_s, k_s, v_s, g_s, gate_s, qin_s, kin_s, a_s, o_s, st_s)

        @pl.when(i == 0)
        def _():
            meta_state[...] = st_s[...]


def _hgrn_tile(layer, h_ref, nw_ref, win_ref, lbl_ref, onw_ref, wout_ref, fw_ref, out_ref,
               q_s, k_s, v_s, g_s, gate_s, qin_s, kin_s, a_s, o_s, st_s):
    logits = lbl_ref[...]
    e = jnp.exp(logits - jnp.max(logits, axis=0, keepdims=True))
    p = e / jnp.sum(e, axis=0, keepdims=True)
    lb = jnp.sum(p[0:layer + 1], axis=0, keepdims=True) - p[0:1]

    h = h_ref[0]
    hn = _rmsnorm(h, nw_ref[...]).astype(_BF16)
    q_s[...] = _silu(_dot(hn, win_ref[:, 0:E_HGRN]))
    fp = _dot(hn, win_ref[:, E_HGRN:2 * E_HGRN])
    t = jnp.exp2(jnp.abs(fp) * (-LOG2_E))
    r = 1.0 / (1.0 + t)
    tr = t * r
    sig_pos = jnp.where(fp >= 0, r, tr)
    sig_neg = jnp.where(fp >= 0, tr, r)
    g_s[...] = jnp.log2(lb + (1.0 - lb) * sig_pos)
    k_s[...] = (1.0 - lb) * sig_neg
    v_s[...] = _dot(hn, win_ref[:, 2 * E_HGRN:3 * E_HGRN]).astype(_BF16)
    gate_s[...] = _silu(_dot(hn, win_ref[:, 3 * E_HGRN:4 * E_HGRN]))

    row = lax.broadcasted_iota(jnp.int32, (CHUNK, CHUNK), 0)
    col = lax.broadcasted_iota(jnp.int32, (CHUNK, CHUNK), 1)
    causal = col <= row
    tri = causal.astype(_BF16)
    tri2 = jnp.concatenate([tri, tri], axis=1)

    d_start_mid, d_mid_end, d_start_end = [], [], []
    for c in range(ROW_TILE // CHUNK):
        rs = slice(c * CHUNK, (c + 1) * CHUNK)
        g = g_s[rs, :]
        g_hi = g.astype(_BF16)
        g_lo = (g - g_hi.astype(_F32)).astype(_BF16)
        b = _dot(tri2, jnp.concatenate([g_hi, g_lo], axis=0))
        b_mid = b[HALF - 1:HALF, :]
        b_end = b[CHUNK - 1:CHUNK, :]
        qin_s[rs, :] = (q_s[rs, :] * jnp.exp2(b - b_mid)).astype(_BF16)
        kin_s[rs, :] = (k_s[rs, :] * jnp.exp2(b_mid - b)).astype(_BF16)
        d_start_mid.append(jnp.exp2(b_mid))
        d_mid_end.append(jnp.exp2(b_end - b_mid))
        d_start_end.append(jnp.exp2(b_end))

    for c in range(ROW_TILE // CHUNK):
        rs = slice(c * CHUNK, (c + 1) * CHUNK)
        for hd in range(N_HEADS):
            ls = slice(hd * HEAD_DIM, (hd + 1) * HEAD_DIM)
            a = _dot_nt(qin_s[rs, ls], kin_s[rs, ls])
            a_s[rs, ls] = jnp.where(causal, a, 0.0).astype(_BF16)

    for c in range(ROW_TILE // CHUNK):
        rs = slice(c * CHUNK, (c + 1) * CHUNK)
        for hd in range(N_HEADS):
            ls = slice(hd * HEAD_DIM, (hd + 1) * HEAD_DIM)
            st = st_s[hd]
            v = v_s[rs, ls]
            k_mid = kin_s[rs, ls]
            s_mid = jnp.transpose(st * d_start_mid[c][:, ls]).astype(_BF16)
            lhs = jnp.concatenate([qin_s[rs, ls], a_s[rs, ls]], axis=1)
            rhs = jnp.concatenate([s_mid, v], axis=0)
            o_s[rs, ls] = _dot(lhs, rhs)
            st_s[hd] = st * d_start_end[c][:, ls] + _dot_tn(v, k_mid) * d_mid_end[c][:, ls]

    parts = []
    for hd in range(N_HEADS):
        ls = slice(hd * HEAD_DIM, (hd + 1) * HEAD_DIM)
        o = o_s[:, ls]
        parts.append(o * lax.rsqrt(jnp.mean(o * o, axis=-1, keepdims=True) + EPS))
    o = jnp.concatenate(parts, axis=-1) * onw_ref[...]
    y = _dot((o * gate_s[...]).astype(_BF16), wout_ref[...])
    out_ref[0] = _rmsnorm(h + y, fw_ref[...])


def _const_spec(shape):
    zeros = (0,) * len(shape)
    return pl.BlockSpec(shape, lambda b, i: zeros, pipeline_mode=pl.Buffered(1))


def kernel(x, meta_tokens, norm_w, pool_w_in, pool_w_grp, pool_scale, pool_w_out, hgrn_w_in, hgrn_lb_logits,
           hgrn_o_norm, hgrn_w_out, final_norm_w):
    batch, seq, d = x.shape
    depth = norm_w.shape[0]
    assert d == D_MODEL and seq % POOL_TILE == 0 and POOL_TILE % ROW_TILE == 0 and depth == 2
    assert meta_tokens.shape == (N_META, D_MODEL)
    params = pltpu.CompilerParams(dimension_semantics=("arbitrary", "arbitrary"),
                                  vmem_limit_bytes=VMEM_LIMIT_BYTES)

    meta_pad = jnp.concatenate([jnp.zeros((POOL_PAD, D_MODEL), x.dtype), meta_tokens.astype(x.dtype)], axis=0)
    pool_tiles = seq // POOL_TILE + 1

    def tile_spec(rows, first):
        return pl.BlockSpec((1, rows, D_MODEL), lambda b, i: (b, jnp.maximum(i + first, 0), 0))

    h1 = pl.pallas_call(
        _pool_layer_kernel,
        grid=(batch, pool_tiles),
        in_specs=[
            _const_spec((POOL_TILE, D_MODEL)),
            tile_spec(POOL_TILE, -1),
            _const_spec((1, D_MODEL)),
            _const_spec((D_MODEL, 2 * E_POOL)),
            _const_spec((len(POOL_WINDOWS), POOL_GROUP_DIM, POOL_GROUP_DIM)),
            _const_spec((1, E_POOL)),
            _const_spec((E_POOL, D_MODEL)),
        ],
        out_specs=tile_spec(POOL_TILE, 0),
        out_shape=jax.ShapeDtypeStruct((batch, pool_tiles * POOL_TILE, D_MODEL), _F32),
        scratch_shapes=[
            pltpu.VMEM((POOL_TILE + MAX_WINDOW, E_POOL), _F32),
            pltpu.VMEM((MAX_WINDOW, E_POOL), _F32),
        ],
        compiler_params=params,
        name="pool_layer",
    )(meta_pad, x, norm_w[0:1], pool_w_in[0].astype(_BF16), pool_w_grp[0].astype(_BF16), pool_scale[0:1],
      pool_w_out[0].astype(_BF16))

    out = pl.pallas_call(
        functools.partial(_hgrn_layer_kernel, 1),
        grid=(batch, seq // ROW_TILE + 1),
        in_specs=[
            tile_spec(ROW_TILE, POOL_TILE // ROW_TILE - 1),
            _const_spec((1, D_MODEL)),
            _const_spec((D_MODEL, 4 * E_HGRN)),
            _const_spec((depth, E_HGRN)),
            _const_spec((1, E_HGRN)),
            _const_spec((E_HGRN, D_MODEL)),
            _const_spec((1, D_MODEL)),
        ],
        out_specs=tile_spec(ROW_TILE, -1),
        out_shape=jax.ShapeDtypeStruct((batch, seq, D_MODEL), _F32),
        scratch_shapes=[
            pltpu.VMEM((ROW_TILE, E_HGRN), _F32),
            pltpu.VMEM((ROW_TILE, E_HGRN), _F32),
            pltpu.VMEM((ROW_TILE, E_HGRN), _BF16),
            pltpu.VMEM((ROW_TILE, E_HGRN), _F32),
            pltpu.VMEM((ROW_TILE, E_HGRN), _F32),
            pltpu.VMEM((ROW_TILE, E_HGRN), _BF16),
            pltpu.VMEM((ROW_TILE, E_HGRN), _BF16),
            pltpu.VMEM((ROW_TILE, E_HGRN), _BF16),
            pltpu.VMEM((ROW_TILE, E_HGRN), _F32),
            pltpu.VMEM((N_HEADS, HEAD_DIM, HEAD_DIM), _F32),
            pltpu.VMEM((N_HEADS, HEAD_DIM, HEAD_DIM), _F32),
        ],
        compiler_params=params,
        name="hgrn_layer",
    )(h1, norm_w[1:2], hgrn_w_in[0].astype(_BF16), hgrn_lb_logits, hgrn_o_norm[0:1], hgrn_w_out[0].astype(_BF16),
      final_norm_w.reshape(1, D_MODEL))
    return out
```

```python
import functools

import jax
import jax.numpy as jnp
from jax import lax
from jax.experimental import pallas as pl
from jax.experimental.pallas import tpu as pltpu

D_MODEL = 1024
N_META = 16
E_POOL = 2048
POOL_WINDOWS = (2, 4, 8, 16)
POOL_GROUP_DIM = E_POOL // len(POOL_WINDOWS)
MAX_WINDOW = max(POOL_WINDOWS)
E_HGRN = 2048
HEAD_DIM = 128
N_HEADS = E_HGRN // HEAD_DIM
EPS = 1e-6
LOG2_E = 1.4426950408889634

POOL_TILE = 512
POOL_PAD = POOL_TILE - N_META
ROW_TILE = 256
CHUNK = 128
HALF = CHUNK // 2
VMEM_LIMIT_BYTES = 56 * 1024 * 1024

_F32 = jnp.float32
_BF16 = jnp.bfloat16


def _rmsnorm(x, w):
    ms = jnp.mean(x * x, axis=-1, keepdims=True)
    return x * lax.rsqrt(ms + EPS) * w


def _silu(x):
    half = 0.5 * x
    return half + half * jnp.tanh(half)


def _dot(a, b):
    return jnp.dot(a, b, preferred_element_type=_F32)


def _dot_nt(a, b):
    return lax.dot_general(a, b, (((1,), (1,)), ((), ())), preferred_element_type=_F32)


def _dot_tn(a, b):
    return lax.dot_general(a, b, (((0,), (0,)), ((), ())), preferred_element_type=_F32)


def _pool_layer_kernel(meta_ref, x_ref, nw_ref, win_ref, wgrp_ref, scale_ref, wout_ref, out_ref, vbuf, meta_carry):
    row = pl.program_id(0)
    i = pl.program_id(1)
    reuse_meta = (i == 0) & (row > 0)

    @pl.when(reuse_meta)
    def _():
        vbuf[0:MAX_WINDOW, :] = meta_carry[...]
        out_ref[0] = jnp.zeros((POOL_TILE, D_MODEL), _F32)

    @pl.when(jnp.logical_not(reuse_meta))
    def _():
        @pl.when(i == 0)
        def _():
            vbuf[0:MAX_WINDOW, :] = jnp.zeros((MAX_WINDOW, E_POOL), _F32)

        h = jnp.where(i == 0, meta_ref[...], x_ref[0])
        hn = _rmsnorm(h, nw_ref[...]).astype(_BF16)
        v = _dot(hn, win_ref[:, 0:E_POOL])
        gate = _dot(hn, win_ref[:, E_POOL:2 * E_POOL])
        vbuf[MAX_WINDOW:MAX_WINDOW + POOL_TILE, :] = v

        pos = i * POOL_TILE - POOL_PAD + lax.broadcasted_iota(jnp.int32, (POOL_TILE, 1), 0)

        parts = []
        for g, w in enumerate(POOL_WINDOWS):
            c0 = g * POOL_GROUP_DIM
            cols = slice(c0, c0 + POOL_GROUP_DIM)

            def rows(shift, cols=cols):
                return vbuf[MAX_WINDOW - shift:MAX_WINDOW - shift + POOL_TILE, cols]

            if w > 8:
                acc = rows(0) + rows(8)
                for j in range(1, 8):
                    acc = acc + (rows(j) + rows(j + 8))
            else:
                acc = rows(0)
                for j in range(1, w):
                    acc = acc + rows(j)
            cnt = jnp.clip(pos + 1, 1, w).astype(_F32)
            u = acc * (1.0 / cnt) - rows(0)
            parts.append(_dot(u.astype(_BF16), wgrp_ref[g]))
        u = jnp.concatenate(parts, axis=-1) * scale_ref[...]
        y = _dot((u * _silu(gate)).astype(_BF16), wout_ref[...])
        out_ref[0] = h + y

        vbuf[0:MAX_WINDOW, :] = vbuf[POOL_TILE:POOL_TILE + MAX_WINDOW, :]

        @pl.when(i == 0)
        def _():
            meta_carry[...] = vbuf[0:MAX_WINDOW, :]


def _hgrn_layer_kernel(layer, h_ref, nw_ref, win_ref, lbl_ref, onw_ref, wout_ref, fw_ref, out_ref,
                       lb_s, q_s, k_s, v_s, g_s, gate_s, qin_s, kin_s, a_s, o_s, st_s, meta_state):
    row = pl.program_id(0)
    i = pl.program_id(1)
    reuse_meta = (i == 0) & (row > 0)

    @pl.when((i == 0) & (row == 0))
    def _():
        logits = lbl_ref[...]
        e = jnp.exp(logits - jnp.max(logits, axis=0, keepdims=True))
        p = e / jnp.sum(e, axis=0, keepdims=True)
        lb_s[...] = jnp.sum(p[0:layer + 1], axis=0, keepdims=True) - p[0:1]

    @pl.when(reuse_meta)
    def _():
        st_s[...] = meta_state[...]

    @pl.when(jnp.logical_not(reuse_meta))
    def _():
        @pl.when(i == 0)
        def _():
            st_s[...] = jnp.zeros(st_s.shape, _F32)

        _hgrn_tile(h_ref, nw_ref, win_ref, onw_ref, wout_ref, fw_ref, out_ref,
                   lb_s, q_s, k_s, v_s, g_s, gate_s, qin_s, kin_s, a_s, o_s, st_s)

        @pl.when(i == 0)
        def _():
            meta_state[...] = st_s[...]


def _hgrn_tile(h_ref, nw_ref, win_ref, onw_ref, wout_ref, fw_ref, out_ref,
               lb_s, q_s, k_s, v_s, g_s, gate_s, qin_s, kin_s, a_s, o_s, st_s):
    lb = lb_s[...]

    h = h_ref[0]
    hn = _rmsnorm(h, nw_ref[...]).astype(_BF16)
    q_s[...] = _silu(_dot(hn, win_ref[:, 0:E_HGRN]))
    fp = _dot(hn, win_ref[:, E_HGRN:2 * E_HGRN])
    t = jnp.exp2(jnp.abs(fp) * (-LOG2_E))
    r = 1.0 / (1.0 + t)
    tr = t * r
    sig_pos = jnp.where(fp >= 0, r, tr)
    sig_neg = jnp.where(fp >= 0, tr, r)
    g_s[...] = jnp.log2(lb + (1.0 - lb) * sig_pos)
    k_s[...] = (1.0 - lb) * sig_neg
    v_s[...] = _dot(hn, win_ref[:, 2 * E_HGRN:3 * E_HGRN]).astype(_BF16)
    gate_s[...] = _silu(_dot(hn, win_ref[:, 3 * E_HGRN:4 * E_HGRN]))

    row = lax.broadcasted_iota(jnp.int32, (CHUNK, CHUNK), 0)
    col = lax.broadcasted_iota(jnp.int32, (CHUNK, CHUNK), 1)
    causal = col <= row
    tri = causal.astype(_BF16)
    tri2 = jnp.concatenate([tri, tri], axis=1)

    d_start_mid, d_mid_end, d_start_end = [], [], []
    for c in range(ROW_TILE // CHUNK):
        rs = slice(c * CHUNK, (c + 1) * CHUNK)
        g = g_s[rs, :]
        g_hi = g.astype(_BF16)
        g_lo = (g - g_hi.astype(_F32)).astype(_BF16)
        b = _dot(tri2, jnp.concatenate([g_hi, g_lo], axis=0))
        b_mid = b[HALF - 1:HALF, :]
        b_end = b[CHUNK - 1:CHUNK, :]
        qin_s[rs, :] = (q_s[rs, :] * jnp.exp2(b - b_mid)).astype(_BF16)
        kin_s[rs, :] = (k_s[rs, :] * jnp.exp2(b_mid - b)).astype(_BF16)
        d_start_mid.append(jnp.exp2(b_mid))
        d_mid_end.append(jnp.exp2(b_end - b_mid))
        d_start_end.append(jnp.exp2(b_end))

    for c in range(ROW_TILE // CHUNK):
        rs = slice(c * CHUNK, (c + 1) * CHUNK)
        for hd in range(N_HEADS):
            ls = slice(hd * HEAD_DIM, (hd + 1) * HEAD_DIM)
            a = _dot_nt(qin_s[rs, ls], kin_s[rs, ls])
            a_s[rs, ls] = jnp.where(causal, a, 0.0).astype(_BF16)

    for c in range(ROW_TILE // CHUNK):
        rs = slice(c * CHUNK, (c + 1) * CHUNK)
        for hd in range(N_HEADS):
            ls = slice(hd * HEAD_DIM, (hd + 1) * HEAD_DIM)
            st = st_s[hd]
            v = v_s[rs, ls]
            k_mid = kin_s[rs, ls]
            s_mid = jnp.transpose(st * d_start_mid[c][:, ls]).astype(_BF16)
            lhs = jnp.concatenate([qin_s[rs, ls], a_s[rs, ls]], axis=1)
            rhs = jnp.concatenate([s_mid, v], axis=0)
            o_s[rs, ls] = _dot(lhs, rhs)
            st_s[hd] = st * d_start_end[c][:, ls] + _dot_tn(v, k_mid) * d_mid_end[c][:, ls]

    parts = []
    for hd in range(N_HEADS):
        ls = slice(hd * HEAD_DIM, (hd + 1) * HEAD_DIM)
        o = o_s[:, ls]
        parts.append(o * lax.rsqrt(jnp.mean(o * o, axis=-1, keepdims=True) + EPS))
    o = jnp.concatenate(parts, axis=-1) * onw_ref[...]
    y = _dot((o * gate_s[...]).astype(_BF16), wout_ref[...])
    out_ref[0] = _rmsnorm(h + y, fw_ref[...])


def _const_spec(shape):
    zeros = (0,) * len(shape)
    return pl.BlockSpec(shape, lambda b, i: zeros, pipeline_mode=pl.Buffered(1))


def kernel(x, meta_tokens, norm_w, pool_w_in, pool_w_grp, pool_scale, pool_w_out, hgrn_w_in, hgrn_lb_logits,
           hgrn_o_norm, hgrn_w_out, final_norm_w):
    batch, seq, d = x.shape
    depth = norm_w.shape[0]
    assert d == D_MODEL and seq % POOL_TILE == 0 and POOL_TILE % ROW_TILE == 0 and depth == 2
    assert meta_tokens.shape == (N_META, D_MODEL)
    params = pltpu.CompilerParams(dimension_semantics=("arbitrary", "arbitrary"),
                                  vmem_limit_bytes=VMEM_LIMIT_BYTES)

    meta_pad = jnp.concatenate([jnp.zeros((POOL_PAD, D_MODEL), x.dtype), meta_tokens.astype(x.dtype)], axis=0)
    pool_tiles = seq // POOL_TILE + 1

    def tile_spec(rows, first):
        return pl.BlockSpec((1, rows, D_MODEL), lambda b, i: (b, jnp.maximum(i + first, 0), 0))

    h1 = pl.pallas_call(
        _pool_layer_kernel,
        grid=(batch, pool_tiles),
        in_specs=[
            _const_spec((POOL_TILE, D_MODEL)),
            tile_spec(POOL_TILE, -1),
            _const_spec((1, D_MODEL)),
            _const_spec((D_MODEL, 2 * E_POOL)),
            _const_spec((len(POOL_WINDOWS), POOL_GROUP_DIM, POOL_GROUP_DIM)),
            _const_spec((1, E_POOL)),
            _const_spec((E_POOL, D_MODEL)),
        ],
        out_specs=tile_spec(POOL_TILE, 0),
        out_shape=jax.ShapeDtypeStruct((batch, pool_tiles * POOL_TILE, D_MODEL), _F32),
        scratch_shapes=[
            pltpu.VMEM((POOL_TILE + MAX_WINDOW, E_POOL), _F32),
            pltpu.VMEM((MAX_WINDOW, E_POOL), _F32),
        ],
        compiler_params=params,
        name="pool_layer",
    )(meta_pad, x, norm_w[0:1], pool_w_in[0].astype(_BF16), pool_w_grp[0].astype(_BF16), pool_scale[0:1],
      pool_w_out[0].astype(_BF16))

    out = pl.pallas_call(
        functools.partial(_hgrn_layer_kernel, 1),
        grid=(batch, seq // ROW_TILE + 1),
        in_specs=[
            tile_spec(ROW_TILE, POOL_TILE // ROW_TILE - 1),
            _const_spec((1, D_MODEL)),
            _const_spec((D_MODEL, 4 * E_HGRN)),
            _const_spec((depth, E_HGRN)),
            _const_spec((1, E_HGRN)),
            _const_spec((E_HGRN, D_MODEL)),
            _const_spec((1, D_MODEL)),
        ],
        out_specs=tile_spec(ROW_TILE, -1),
        out_shape=jax.ShapeDtypeStruct((batch, seq, D_MODEL), _F32),
        scratch_shapes=[
            pltpu.VMEM((1, E_HGRN), _F32),
            pltpu.VMEM((ROW_TILE, E_HGRN), _F32),
            pltpu.VMEM((ROW_TILE, E_HGRN), _F32),
            pltpu.VMEM((ROW_TILE, E_HGRN), _BF16),
            pltpu.VMEM((ROW_TILE, E_HGRN), _F32),
            pltpu.VMEM((ROW_TILE, E_HGRN), _F32),
            pltpu.VMEM((ROW_TILE, E_HGRN), _BF16),
            pltpu.VMEM((ROW_TILE, E_HGRN), _BF16),
            pltpu.VMEM((ROW_TILE, E_HGRN), _BF16),
            pltpu.VMEM((ROW_TILE, E_HGRN), _F32),
            pltpu.VMEM((N_HEADS, HEAD_DIM, HEAD_DIM), _F32),
            pltpu.VMEM((N_HEADS, HEAD_DIM, HEAD_DIM), _F32),
        ],
        compiler_params=params,
        name="hgrn_layer",
    )(h1, norm_w[1:2], hgrn_w_in[0].astype(_BF16), hgrn_lb_logits, hgrn_o_norm[0:1], hgrn_w_out[0].astype(_BF16),
      final_norm_w.reshape(1, D_MODEL))
    return out
```

```python
import functools

import jax
import jax.numpy as jnp
from jax import lax
from jax.experimental import pallas as pl
from jax.experimental.pallas import tpu as pltpu

D_MODEL = 1024
N_META = 16
E_POOL = 2048
POOL_WINDOWS = (2, 4, 8, 16)
POOL_GROUP_DIM = E_POOL // len(POOL_WINDOWS)
MAX_WINDOW = max(POOL_WINDOWS)
SUBLANES = 8
POOL_CARRY = SUBLANES * (MAX_WINDOW.bit_length() - 1)
E_HGRN = 2048
HEAD_DIM = 128
N_HEADS = E_HGRN // HEAD_DIM
EPS = 1e-6
LOG2_E = 1.4426950408889634

POOL_TILE = 512
POOL_PAD = POOL_TILE - N_META
ROW_TILE = 256
CHUNK = 128
HALF = CHUNK // 2
VMEM_LIMIT_BYTES = 56 * 1024 * 1024

_F32 = jnp.float32
_BF16 = jnp.bfloat16


def _rmsnorm(x, w):
    ms = jnp.mean(x * x, axis=-1, keepdims=True)
    return x * lax.rsqrt(ms + EPS) * w


def _silu(x):
    half = 0.5 * x
    return half + half * jnp.tanh(half)


def _dot(a, b):
    return jnp.dot(a, b, preferred_element_type=_F32)


def _dot_nt(a, b):
    return lax.dot_general(a, b, (((1,), (1,)), ((), ())), preferred_element_type=_F32)


def _dot_tn(a, b):
    return lax.dot_general(a, b, (((0,), (0,)), ((), ())), preferred_element_type=_F32)


def _pool_layer_kernel(meta_ref, x_ref, nw_ref, win_ref, wgrp_ref, scale_ref, wout_ref, out_ref, vbuf, meta_carry):
    row = pl.program_id(0)
    i = pl.program_id(1)
    weights = (nw_ref, win_ref, wgrp_ref, scale_ref, wout_ref)

    @pl.when((i == 0) & (row == 0))
    def _():
        vbuf[0:POOL_CARRY, :] = jnp.zeros((POOL_CARRY, E_POOL), _F32)
        _pool_tile(lambda: meta_ref[...], True, *weights, out_ref, vbuf)
        meta_carry[...] = vbuf[0:POOL_CARRY, :]

    @pl.when((i == 0) & (row > 0))
    def _():
        vbuf[0:POOL_CARRY, :] = meta_carry[...]
        out_ref[0] = jnp.zeros((POOL_TILE, D_MODEL), _F32)

    @pl.when(i > 0)
    def _():
        _pool_tile(lambda: x_ref[0], False, *weights, out_ref, vbuf)


def _pool_tile(read_h, is_meta_tile, nw_ref, win_ref, wgrp_ref, scale_ref, wout_ref, out_ref, vbuf):
    hn = _rmsnorm(read_h(), nw_ref[...]).astype(_BF16)
    v = _dot(hn, win_ref[:, 0:E_POOL])
    gate = _dot(hn, win_ref[:, E_POOL:2 * E_POOL])
    vbuf[POOL_CARRY:POOL_CARRY + POOL_TILE, :] = v

    parts = []
    for g, w in enumerate(POOL_WINDOWS):
        c0 = g * POOL_GROUP_DIM
        cols = slice(c0, c0 + POOL_GROUP_DIM)

        n_stages = w.bit_length() - 1
        acc = vbuf[POOL_CARRY - SUBLANES * n_stages:POOL_CARRY + POOL_TILE, cols]
        for stage in range(n_stages):
            m = 1 << stage
            acc = acc[SUBLANES:] + acc[SUBLANES - m:acc.shape[0] - m]
        if is_meta_tile:
            pos = lax.broadcasted_iota(jnp.int32, (POOL_TILE, 1), 0) - POOL_PAD
            inv_cnt = 1.0 / jnp.clip(pos + 1, 1, w).astype(_F32)
        else:
            inv_cnt = 1.0 / w
        u = acc * inv_cnt - vbuf[POOL_CARRY:POOL_CARRY + POOL_TILE, cols]
        parts.append(_dot(u.astype(_BF16), wgrp_ref[g]))
    u = jnp.concatenate(parts, axis=-1) * scale_ref[...]
    y = _dot((u * _silu(gate)).astype(_BF16), wout_ref[...])
    out_ref[0] = read_h() + y

    vbuf[0:POOL_CARRY, :] = vbuf[POOL_TILE:POOL_TILE + POOL_CARRY, :]


def _hgrn_layer_kernel(layer, h_ref, nw_ref, win_ref, lbl_ref, onw_ref, wout_ref, fw_ref, out_ref,
                       lb_s, q_s, k_s, v_s, g_s, gate_s, qin_s, kin_s, a_s, o_s, st_s, meta_state):
    row = pl.program_id(0)
    i = pl.program_id(1)
    reuse_meta = (i == 0) & (row > 0)

    @pl.when((i == 0) & (row == 0))
    def _():
        logits = lbl_ref[...]
        e = jnp.exp(logits - jnp.max(logits, axis=0, keepdims=True))
        p = e / jnp.sum(e, axis=0, keepdims=True)
        lb_s[...] = jnp.sum(p[0:layer + 1], axis=0, keepdims=True) - p[0:1]

    @pl.when(reuse_meta)
    def _():
        st_s[...] = meta_state[...]

    @pl.when(jnp.logical_not(reuse_meta))
    def _():
        @pl.when(i == 0)
        def _():
            st_s[...] = jnp.zeros(st_s.shape, _F32)

        _hgrn_tile(h_ref, nw_ref, win_ref, onw_ref, wout_ref, fw_ref, out_ref,
                   lb_s, q_s, k_s, v_s, g_s, gate_s, qin_s, kin_s, a_s, o_s, st_s)

        @pl.when(i == 0)
        def _():
            meta_state[...] = st_s[...]


def _hgrn_tile(h_ref, nw_ref, win_ref, onw_ref, wout_ref, fw_ref, out_ref,
               lb_s, q_s, k_s, v_s, g_s, gate_s, qin_s, kin_s, a_s, o_s, st_s):
    lb = lb_s[...]

    hn = _rmsnorm(h_ref[0], nw_ref[...]).astype(_BF16)
    q_s[...] = _silu(_dot(hn, win_ref[:, 0:E_HGRN]))
    fp = _dot(hn, win_ref[:, E_HGRN:2 * E_HGRN])
    t = jnp.exp2(jnp.abs(fp) * (-LOG2_E))
    r = 1.0 / (1.0 + t)
    tr = t * r
    sig_pos = jnp.where(fp >= 0, r, tr)
    sig_neg = jnp.where(fp >= 0, tr, r)
    g_s[...] = jnp.log2(lb + (1.0 - lb) * sig_pos)
    k_s[...] = (1.0 - lb) * sig_neg
    v_s[...] = _dot(hn, win_ref[:, 2 * E_HGRN:3 * E_HGRN]).astype(_BF16)
    gate_s[...] = _silu(_dot(hn, win_ref[:, 3 * E_HGRN:4 * E_HGRN]))

    row = lax.broadcasted_iota(jnp.int32, (CHUNK, CHUNK), 0)
    col = lax.broadcasted_iota(jnp.int32, (CHUNK, CHUNK), 1)
    causal = col <= row
    tri = causal.astype(_BF16)
    tri2 = jnp.concatenate([tri, tri], axis=1)

    d_start_mid, d_mid_end, d_start_end = [], [], []
    for c in range(ROW_TILE // CHUNK):
        rs = slice(c * CHUNK, (c + 1) * CHUNK)
        g = g_s[rs, :]
        g_hi = g.astype(_BF16)
        g_lo = (g - g_hi.astype(_F32)).astype(_BF16)
        b = _dot(tri2, jnp.concatenate([g_hi, g_lo], axis=0))
        b_mid = b[HALF - 1:HALF, :]
        b_end = b[CHUNK - 1:CHUNK, :]
        qin_s[rs, :] = (q_s[rs, :] * jnp.exp2(b - b_mid)).astype(_BF16)
        kin_s[rs, :] = (k_s[rs, :] * jnp.exp2(b_mid - b)).astype(_BF16)
        d_start_mid.append(jnp.exp2(b_mid))
        d_mid_end.append(jnp.exp2(b_end - b_mid))
        d_start_end.append(jnp.exp2(b_end))

    for c in range(ROW_TILE // CHUNK):
        rs = slice(c * CHUNK, (c + 1) * CHUNK)
        for hd in range(N_HEADS):
            ls = slice(hd * HEAD_DIM, (hd + 1) * HEAD_DIM)
            a = _dot_nt(qin_s[rs, ls], kin_s[rs, ls])
            a_s[rs, ls] = jnp.where(causal, a, 0.0).astype(_BF16)

    for c in range(ROW_TILE // CHUNK):
        rs = slice(c * CHUNK, (c + 1) * CHUNK)
        for hd in range(N_HEADS):
            ls = slice(hd * HEAD_DIM, (hd + 1) * HEAD_DIM)
            st = st_s[hd]
            v = v_s[rs, ls]
            k_mid = kin_s[rs, ls]
            s_mid = jnp.transpose(st * d_start_mid[c][:, ls]).astype(_BF16)
            lhs = jnp.concatenate([qin_s[rs, ls], a_s[rs, ls]], axis=1)
            rhs = jnp.concatenate([s_mid, v], axis=0)
            o_s[rs, ls] = _dot(lhs, rhs)
            st_s[hd] = st * d_start_end[c][:, ls] + _dot_tn(v, k_mid) * d_mid_end[c][:, ls]

    parts = []
    for hd in range(N_HEADS):
        ls = slice(hd * HEAD_DIM, (hd + 1) * HEAD_DIM)
        o = o_s[:, ls]
        parts.append(o * lax.rsqrt(jnp.mean(o * o, axis=-1, keepdims=True) + EPS))
    o = jnp.concatenate(parts, axis=-1) * onw_ref[...]
    y = _dot((o * gate_s[...]).astype(_BF16), wout_ref[...])
    out_ref[0] = _rmsnorm(h_ref[0] + y, fw_ref[...])


def _const_spec(shape):
    zeros = (0,) * len(shape)
    return pl.BlockSpec(shape, lambda b, i: zeros, pipeline_mode=pl.Buffered(1))


def kernel(x, meta_tokens, norm_w, pool_w_in, pool_w_grp, pool_scale, pool_w_out, hgrn_w_in, hgrn_lb_logits,
           hgrn_o_norm, hgrn_w_out, final_norm_w):
    batch, seq, d = x.shape
    depth = norm_w.shape[0]
    assert d == D_MODEL and seq % POOL_TILE == 0 and POOL_TILE % ROW_TILE == 0 and depth == 2
    assert meta_tokens.shape == (N_META, D_MODEL) and N_META >= MAX_WINDOW - 1
    params = pltpu.CompilerParams(dimension_semantics=("arbitrary", "arbitrary"),
                                  vmem_limit_bytes=VMEM_LIMIT_BYTES)

    meta_pad = jnp.concatenate([jnp.zeros((POOL_PAD, D_MODEL), x.dtype), meta_tokens.astype(x.dtype)], axis=0)
    pool_tiles = seq // POOL_TILE + 1

    def tile_spec(rows, first):
        return pl.BlockSpec((1, rows, D_MODEL), lambda b, i: (b, jnp.maximum(i + first, 0), 0))

    h1 = pl.pallas_call(
        _pool_layer_kernel,
        grid=(batch, pool_tiles),
        in_specs=[
            _const_spec((POOL_TILE, D_MODEL)),
            tile_spec(POOL_TILE, -1),
            _const_spec((1, D_MODEL)),
            _const_spec((D_MODEL, 2 * E_POOL)),
            _const_spec((len(POOL_WINDOWS), POOL_GROUP_DIM, POOL_GROUP_DIM)),
            _const_spec((1, E_POOL)),
            _const_spec((E_POOL, D_MODEL)),
        ],
        out_specs=tile_spec(POOL_TILE, 0),
        out_shape=jax.ShapeDtypeStruct((batch, pool_tiles * POOL_TILE, D_MODEL), _F32),
        scratch_shapes=[
            pltpu.VMEM((POOL_TILE + POOL_CARRY, E_POOL), _F32),
            pltpu.VMEM((POOL_CARRY, E_POOL), _F32),
        ],
        compiler_params=params,
        name="pool_layer",
    )(meta_pad, x, norm_w[0:1], pool_w_in[0].astype(_BF16), pool_w_grp[0].astype(_BF16), pool_scale[0:1],
      pool_w_out[0].astype(_BF16))

    out = pl.pallas_call(
        functools.partial(_hgrn_layer_kernel, 1),
        grid=(batch, seq // ROW_TILE + 1),
        in_specs=[
            tile_spec(ROW_TILE, POOL_TILE // ROW_TILE - 1),
            _const_spec((1, D_MODEL)),
            _const_spec((D_MODEL, 4 * E_HGRN)),
            _const_spec((depth, E_HGRN)),
            _const_spec((1, E_HGRN)),
            _const_spec((E_HGRN, D_MODEL)),
            _const_spec((1, D_MODEL)),
        ],
        out_specs=tile_spec(ROW_TILE, -1),
        out_shape=jax.ShapeDtypeStruct((batch, seq, D_MODEL), _F32),
        scratch_shapes=[
            pltpu.VMEM((1, E_HGRN), _F32),
            pltpu.VMEM((ROW_TILE, E_HGRN), _F32),
            pltpu.VMEM((ROW_TILE, E_HGRN), _F32),
            pltpu.VMEM((ROW_TILE, E_HGRN), _BF16),
            pltpu.VMEM((ROW_TILE, E_HGRN), _F32),
            pltpu.VMEM((ROW_TILE, E_HGRN), _F32),
            pltpu.VMEM((ROW_TILE, E_HGRN), _BF16),
            pltpu.VMEM((ROW_TILE, E_HGRN), _BF16),
            pltpu.VMEM((ROW_TILE, E_HGRN), _BF16),
            pltpu.VMEM((ROW_TILE, E_HGRN), _F32),
            pltpu.VMEM((N_HEADS, HEAD_DIM, HEAD_DIM), _F32),
            pltpu.VMEM((N_HEADS, HEAD_DIM, HEAD_DIM), _F32),
        ],
        compiler_params=params,
        name="hgrn_layer",
    )(h1, norm_w[1:2], hgrn_w_in[0].astype(_BF16), hgrn_lb_logits, hgrn_o_norm[0:1], hgrn_w_out[0].astype(_BF16),
      final_norm_w.reshape(1, D_MODEL))
    return out
```

```python
import functools

import jax
import jax.numpy as jnp
from jax import lax
from jax.experimental import pallas as pl
from jax.experimental.pallas import tpu as pltpu

D_MODEL = 1024
N_META = 16
E_POOL = 2048
POOL_WINDOWS = (2, 4, 8, 16)
POOL_GROUP_DIM = E_POOL // len(POOL_WINDOWS)
MAX_WINDOW = max(POOL_WINDOWS)
SUBLANES = 8
POOL_CARRY = SUBLANES * (MAX_WINDOW.bit_length() - 1)
E_HGRN = 2048
HEAD_DIM = 128
N_HEADS = E_HGRN // HEAD_DIM
EPS = 1e-6
LOG2_E = 1.4426950408889634

POOL_TILE = 512
POOL_PAD = POOL_TILE - N_META
ROW_TILE = 256
CHUNK = 128
HALF = CHUNK // 2
MAX_LOG2_GROWTH = 100.0
SMALL_STEP = 16
VMEM_LIMIT_BYTES = 56 * 1024 * 1024

_F32 = jnp.float32
_BF16 = jnp.bfloat16


def _rmsnorm(x, w):
    ms = jnp.mean(x * x, axis=-1, keepdims=True)
    return x * lax.rsqrt(ms + EPS) * w


def _silu(x):
    half = 0.5 * x
    return half + half * jnp.tanh(half)


def _dot(a, b):
    return jnp.dot(a, b, preferred_element_type=_F32)


def _dot_nt(a, b):
    return lax.dot_general(a, b, (((1,), (1,)), ((), ())), preferred_element_type=_F32)


def _dot_tn(a, b):
    return lax.dot_general(a, b, (((0,), (0,)), ((), ())), preferred_element_type=_F32)


def _pool_layer_kernel(meta_ref, x_ref, nw_ref, win_ref, wgrp_ref, scale_ref, wout_ref, out_ref, vbuf, meta_carry):
    row = pl.program_id(0)
    i = pl.program_id(1)
    weights = (nw_ref, win_ref, wgrp_ref, scale_ref, wout_ref)

    @pl.when((i == 0) & (row == 0))
    def _():
        vbuf[0:POOL_CARRY, :] = jnp.zeros((POOL_CARRY, E_POOL), _F32)
        _pool_tile(lambda: meta_ref[...], True, *weights, out_ref, vbuf)
        meta_carry[...] = vbuf[0:POOL_CARRY, :]

    @pl.when((i == 0) & (row > 0))
    def _():
        vbuf[0:POOL_CARRY, :] = meta_carry[...]
        out_ref[0] = jnp.zeros((POOL_TILE, D_MODEL), _F32)

    @pl.when(i > 0)
    def _():
        _pool_tile(lambda: x_ref[0], False, *weights, out_ref, vbuf)


def _pool_tile(read_h, is_meta_tile, nw_ref, win_ref, wgrp_ref, scale_ref, wout_ref, out_ref, vbuf):
    hn = _rmsnorm(read_h(), nw_ref[...]).astype(_BF16)
    v = _dot(hn, win_ref[:, 0:E_POOL])
    gate = _dot(hn, win_ref[:, E_POOL:2 * E_POOL])
    vbuf[POOL_CARRY:POOL_CARRY + POOL_TILE, :] = v

    parts = []
    for g, w in enumerate(POOL_WINDOWS):
        c0 = g * POOL_GROUP_DIM
        cols = slice(c0, c0 + POOL_GROUP_DIM)

        n_stages = w.bit_length() - 1
        acc = vbuf[POOL_CARRY - SUBLANES * n_stages:POOL_CARRY + POOL_TILE, cols]
        for stage in range(n_stages):
            m = 1 << stage
            acc = acc[SUBLANES:] + acc[SUBLANES - m:acc.shape[0] - m]
        if is_meta_tile:
            pos = lax.broadcasted_iota(jnp.int32, (POOL_TILE, 1), 0) - POOL_PAD
            inv_cnt = 1.0 / jnp.clip(pos + 1, 1, w).astype(_F32)
        else:
            inv_cnt = 1.0 / w
        u = acc * inv_cnt - vbuf[POOL_CARRY:POOL_CARRY + POOL_TILE, cols]
        parts.append(_dot(u.astype(_BF16), wgrp_ref[g]))
    u = jnp.concatenate(parts, axis=-1) * scale_ref[...]
    y = _dot((u * _silu(gate)).astype(_BF16), wout_ref[...])
    out_ref[0] = read_h() + y

    vbuf[0:POOL_CARRY, :] = vbuf[POOL_TILE:POOL_TILE + POOL_CARRY, :]


def _hgrn_layer_kernel(layer, h_ref, nw_ref, win_ref, lbl_ref, onw_ref, wout_ref, fw_ref, out_ref,
                       lb_s, q_s, k_s, v_s, g_s, gate_s, qin_s, kin_s, a_s, o_s, st_s, st0_s, meta_state):
    row = pl.program_id(0)
    i = pl.program_id(1)
    reuse_meta = (i == 0) & (row > 0)

    @pl.when((i == 0) & (row == 0))
    def _():
        logits = lbl_ref[...]
        e = jnp.exp(logits - jnp.max(logits, axis=0, keepdims=True))
        p = e / jnp.sum(e, axis=0, keepdims=True)
        lb_s[...] = jnp.sum(p[0:layer + 1], axis=0, keepdims=True) - p[0:1]

    @pl.when(reuse_meta)
    def _():
        st_s[...] = meta_state[...]

    @pl.when(jnp.logical_not(reuse_meta))
    def _():
        @pl.when(i == 0)
        def _():
            st_s[...] = jnp.zeros(st_s.shape, _F32)

        _hgrn_tile(h_ref, nw_ref, win_ref, onw_ref, wout_ref, fw_ref, out_ref,
                   lb_s, q_s, k_s, v_s, g_s, gate_s, qin_s, kin_s, a_s, o_s, st_s, st0_s)

        @pl.when(i == 0)
        def _():
            meta_state[...] = st_s[...]


def _hgrn_tile(h_ref, nw_ref, win_ref, onw_ref, wout_ref, fw_ref, out_ref,
               lb_s, q_s, k_s, v_s, g_s, gate_s, qin_s, kin_s, a_s, o_s, st_s, st0_s):
    lb = lb_s[...]
    st0_s[...] = st_s[...]

    hn = _rmsnorm(h_ref[0], nw_ref[...]).astype(_BF16)
    q_s[...] = _silu(_dot(hn, win_ref[:, 0:E_HGRN]))
    fp = _dot(hn, win_ref[:, E_HGRN:2 * E_HGRN])
    t = jnp.exp2(jnp.abs(fp) * (-LOG2_E))
    r = 1.0 / (1.0 + t)
    tr = t * r
    sig_pos = jnp.where(fp >= 0, r, tr)
    sig_neg = jnp.where(fp >= 0, tr, r)
    g_s[...] = jnp.log2(lb + (1.0 - lb) * sig_pos)
    k_s[...] = (1.0 - lb) * sig_neg
    v_s[...] = _dot(hn, win_ref[:, 2 * E_HGRN:3 * E_HGRN]).astype(_BF16)
    gate_s[...] = _silu(_dot(hn, win_ref[:, 3 * E_HGRN:4 * E_HGRN]))

    row = lax.broadcasted_iota(jnp.int32, (CHUNK, CHUNK), 0)
    col = lax.broadcasted_iota(jnp.int32, (CHUNK, CHUNK), 1)
    causal = col <= row
    tri = causal.astype(_BF16)
    tri2 = jnp.concatenate([tri, tri], axis=1)

    d_start_mid, d_mid_end, d_start_end = [], [], []
    growth = []
    for c in range(ROW_TILE // CHUNK):
        rs = slice(c * CHUNK, (c + 1) * CHUNK)
        g = g_s[rs, :]
        g_hi = g.astype(_BF16)
        g_lo = (g - g_hi.astype(_F32)).astype(_BF16)
        b = _dot(tri2, jnp.concatenate([g_hi, g_lo], axis=0))
        b_mid = b[HALF - 1:HALF, :]
        b_end = b[CHUNK - 1:CHUNK, :]
        qin_s[rs, :] = (q_s[rs, :] * jnp.exp2(b - b_mid)).astype(_BF16)
        kin_s[rs, :] = (k_s[rs, :] * jnp.exp2(b_mid - b)).astype(_BF16)
        d_start_mid.append(jnp.exp2(b_mid))
        d_mid_end.append(jnp.exp2(b_end - b_mid))
        d_start_end.append(jnp.exp2(b_end))
        growth.append(jnp.maximum(b[0:1, :] - b_mid, b_mid - b_end))

    worst = growth[0]
    for gr in growth[1:]:
        worst = jnp.maximum(worst, gr)
    in_range = jnp.max(worst) <= MAX_LOG2_GROWTH

    for c in range(ROW_TILE // CHUNK):
        rs = slice(c * CHUNK, (c + 1) * CHUNK)
        for hd in range(N_HEADS):
            ls = slice(hd * HEAD_DIM, (hd + 1) * HEAD_DIM)
            a = _dot_nt(qin_s[rs, ls], kin_s[rs, ls])
            a_s[rs, ls] = jnp.where(causal, a, 0.0).astype(_BF16)

    for c in range(ROW_TILE // CHUNK):
        rs = slice(c * CHUNK, (c + 1) * CHUNK)
        for hd in range(N_HEADS):
            ls = slice(hd * HEAD_DIM, (hd + 1) * HEAD_DIM)
            st = st_s[hd]
            v = v_s[rs, ls]
            k_mid = kin_s[rs, ls]
            s_mid = jnp.transpose(st * d_start_mid[c][:, ls]).astype(_BF16)
            lhs = jnp.concatenate([qin_s[rs, ls], a_s[rs, ls]], axis=1)
            rhs = jnp.concatenate([s_mid, v], axis=0)
            o_s[rs, ls] = _dot(lhs, rhs)
            st_s[hd] = st * d_start_end[c][:, ls] + _dot_tn(v, k_mid) * d_mid_end[c][:, ls]

    _hgrn_output(h_ref, onw_ref, wout_ref, fw_ref, out_ref, gate_s, o_s)

    @pl.when(jnp.logical_not(in_range))
    def _():
        st_s[...] = st0_s[...]
        _hgrn_recurrence_small_steps(q_s, k_s, v_s, g_s, o_s, st_s)
        _hgrn_output(h_ref, onw_ref, wout_ref, fw_ref, out_ref, gate_s, o_s)


def _hgrn_output(h_ref, onw_ref, wout_ref, fw_ref, out_ref, gate_s, o_s):
    parts = []
    for hd in range(N_HEADS):
        ls = slice(hd * HEAD_DIM, (hd + 1) * HEAD_DIM)
        o = o_s[:, ls]
        parts.append(o * lax.rsqrt(jnp.mean(o * o, axis=-1, keepdims=True) + EPS))
    o = jnp.concatenate(parts, axis=-1) * onw_ref[...]
    y = _dot((o * gate_s[...]).astype(_BF16), wout_ref[...])
    out_ref[0] = _rmsnorm(h_ref[0] + y, fw_ref[...])


def _hgrn_recurrence_small_steps(q_s, k_s, v_s, g_s, o_s, st_s):
    step_row = lax.broadcasted_iota(jnp.int32, (SMALL_STEP, 1), 0)
    r_i = lax.broadcasted_iota(jnp.int32, (SMALL_STEP, SMALL_STEP), 0)
    c_i = lax.broadcasted_iota(jnp.int32, (SMALL_STEP, SMALL_STEP), 1)
    tri = (c_i <= r_i).astype(_BF16)
    tri3 = jnp.concatenate([tri, tri, tri], axis=1)

    def body(j, carry):
        rs = pl.ds(pl.multiple_of(j * SMALL_STEP, SMALL_STEP), SMALL_STEP)
        for hd in range(N_HEADS):
            ls = slice(hd * HEAD_DIM, (hd + 1) * HEAD_DIM)
            g = g_s[rs, ls]
            g_hi = g.astype(_BF16)
            g_md = (g - g_hi.astype(_F32)).astype(_BF16)
            g_lo = (g - g_hi.astype(_F32) - g_md.astype(_F32)).astype(_BF16)
            b = _dot(tri3, jnp.concatenate([g_hi, g_md, g_lo], axis=0))
            b_end = b[SMALL_STEP - 1:SMALL_STEP, :]
            q = q_s[rs, ls]
            k = k_s[rs, ls]
            v = v_s[rs, ls].astype(_F32)
            st = st_s[hd]
            o = _dot_nt((q * jnp.exp2(b)).astype(_BF16), st.astype(_BF16))
            rows = []
            for t in range(SMALL_STEP):
                rel = jnp.where(step_row <= t, b[t:t + 1, :] - b, -jnp.inf)
                score = jnp.sum(q[t:t + 1, :] * k * jnp.exp2(rel), axis=-1, keepdims=True)
                rows.append(jnp.sum(score * v, axis=0, keepdims=True))
            o_s[rs, ls] = o + jnp.concatenate(rows, axis=0)
            k_end = (k * jnp.exp2(b_end - b)).astype(_BF16)
            st_s[hd] = st * jnp.exp2(b_end) + _dot_tn(v.astype(_BF16), k_end)
        return carry

    lax.fori_loop(0, ROW_TILE // SMALL_STEP, body, 0)


def _const_spec(shape):
    zeros = (0,) * len(shape)
    return pl.BlockSpec(shape, lambda b, i: zeros, pipeline_mode=pl.Buffered(1))


def kernel(x, meta_tokens, norm_w, pool_w_in, pool_w_grp, pool_scale, pool_w_out, hgrn_w_in, hgrn_lb_logits,
           hgrn_o_norm, hgrn_w_out, final_norm_w):
    batch, seq, d = x.shape
    depth = norm_w.shape[0]
    assert d == D_MODEL and seq % POOL_TILE == 0 and POOL_TILE % ROW_TILE == 0 and depth == 2
    assert meta_tokens.shape == (N_META, D_MODEL) and N_META >= MAX_WINDOW - 1
    params = pltpu.CompilerParams(dimension_semantics=("arbitrary", "arbitrary"),
                                  vmem_limit_bytes=VMEM_LIMIT_BYTES)

    meta_pad = jnp.concatenate([jnp.zeros((POOL_PAD, D_MODEL), x.dtype), meta_tokens.astype(x.dtype)], axis=0)
    pool_tiles = seq // POOL_TILE + 1

    def tile_spec(rows, first):
        return pl.BlockSpec((1, rows, D_MODEL), lambda b, i: (b, jnp.maximum(i + first, 0), 0))

    h1 = pl.pallas_call(
        _pool_layer_kernel,
        grid=(batch, pool_tiles),
        in_specs=[
            _const_spec((POOL_TILE, D_MODEL)),
            tile_spec(POOL_TILE, -1),
            _const_spec((1, D_MODEL)),
            _const_spec((D_MODEL, 2 * E_POOL)),
            _const_spec((len(POOL_WINDOWS), POOL_GROUP_DIM, POOL_GROUP_DIM)),
            _const_spec((1, E_POOL)),
            _const_spec((E_POOL, D_MODEL)),
        ],
        out_specs=tile_spec(POOL_TILE, 0),
        out_shape=jax.ShapeDtypeStruct((batch, pool_tiles * POOL_TILE, D_MODEL), _F32),
        scratch_shapes=[
            pltpu.VMEM((POOL_TILE + POOL_CARRY, E_POOL), _F32),
            pltpu.VMEM((POOL_CARRY, E_POOL), _F32),
        ],
        compiler_params=params,
        name="pool_layer",
    )(meta_pad, x, norm_w[0:1], pool_w_in[0].astype(_BF16), pool_w_grp[0].astype(_BF16), pool_scale[0:1],
      pool_w_out[0].astype(_BF16))

    out = pl.pallas_call(
        functools.partial(_hgrn_layer_kernel, 1),
        grid=(batch, seq // ROW_TILE + 1),
        in_specs=[
            tile_spec(ROW_TILE, POOL_TILE // ROW_TILE - 1),
            _const_spec((1, D_MODEL)),
            _const_spec((D_MODEL, 4 * E_HGRN)),
            _const_spec((depth, E_HGRN)),
            _const_spec((1, E_HGRN)),
            _const_spec((E_HGRN, D_MODEL)),
            _const_spec((1, D_MODEL)),
        ],
        out_specs=tile_spec(ROW_TILE, -1),
        out_shape=jax.ShapeDtypeStruct((batch, seq, D_MODEL), _F32),
        scratch_shapes=[
            pltpu.VMEM((1, E_HGRN), _F32),
            pltpu.VMEM((ROW_TILE, E_HGRN), _F32),
            pltpu.VMEM((ROW_TILE, E_HGRN), _F32),
            pltpu.VMEM((ROW_TILE, E_HGRN), _BF16),
            pltpu.VMEM((ROW_TILE, E_HGRN), _F32),
            pltpu.VMEM((ROW_TILE, E_HGRN), _F32),
            pltpu.VMEM((ROW_TILE, E_HGRN), _BF16),
            pltpu.VMEM((ROW_TILE, E_HGRN), _BF16),
            pltpu.VMEM((ROW_TILE, E_HGRN), _BF16),
            pltpu.VMEM((ROW_TILE, E_HGRN), _F32),
            pltpu.VMEM((N_HEADS, HEAD_DIM, HEAD_DIM), _F32),
            pltpu.VMEM((N_HEADS, HEAD_DIM, HEAD_DIM), _F32),
            pltpu.VMEM((N_HEADS, HEAD_DIM, HEAD_DIM), _F32),
        ],
        compiler_params=params,
        name="hgrn_layer",
    )(h1, norm_w[1:2], hgrn_w_in[0].astype(_BF16), hgrn_lb_logits, hgrn_o_norm[0:1], hgrn_w_out[0].astype(_BF16),
      final_norm_w.reshape(1, D_MODEL))
    return out
```

```python
import functools

import jax
import jax.numpy as jnp
from jax import lax
from jax.experimental import pallas as pl
from jax.experimental.pallas import tpu as pltpu

D_MODEL = 1024
N_META = 16
E_POOL = 2048
POOL_WINDOWS = (2, 4, 8, 16)
POOL_GROUP_DIM = E_POOL // len(POOL_WINDOWS)
MAX_WINDOW = max(POOL_WINDOWS)
SUBLANES = 8
POOL_CARRY = SUBLANES * (MAX_WINDOW.bit_length() - 1)
E_HGRN = 2048
HEAD_DIM = 128
N_HEADS = E_HGRN // HEAD_DIM
EPS = 1e-6
LOG2_E = 1.4426950408889634

POOL_TILE = 512
POOL_PAD = POOL_TILE - N_META
ROW_TILE = 256
CHUNK = 128
HALF = CHUNK // 2
MAX_LOG2_GROWTH = 100.0
SMALL_STEP = 16
VMEM_LIMIT_BYTES = 56 * 1024 * 1024

_F32 = jnp.float32
_BF16 = jnp.bfloat16


def _rmsnorm(x, w):
    ms = jnp.mean(x * x, axis=-1, keepdims=True)
    return x * lax.rsqrt(ms + EPS) * w


def _silu(x):
    half = 0.5 * x
    return half + half * jnp.tanh(half)


def _dot(a, b):
    return jnp.dot(a, b, preferred_element_type=_F32)


def _dot_nt(a, b):
    return lax.dot_general(a, b, (((1,), (1,)), ((), ())), preferred_element_type=_F32)


def _dot_tn(a, b):
    return lax.dot_general(a, b, (((0,), (0,)), ((), ())), preferred_element_type=_F32)


def _pool_layer_kernel(meta_ref, x_ref, nw_ref, win_ref, wgrp_ref, scale_ref, wout_ref, out_ref, vbuf, meta_carry):
    row = pl.program_id(0)
    i = pl.program_id(1)
    weights = (nw_ref, win_ref, wgrp_ref, scale_ref, wout_ref)

    @pl.when((i == 0) & (row == 0))
    def _():
        vbuf[0:POOL_CARRY, :] = jnp.zeros((POOL_CARRY, E_POOL), _F32)
        _pool_tile(lambda: meta_ref[...], True, *weights, out_ref, vbuf)
        meta_carry[...] = vbuf[0:POOL_CARRY, :]

    @pl.when((i == 0) & (row > 0))
    def _():
        vbuf[0:POOL_CARRY, :] = meta_carry[...]
        out_ref[0] = jnp.zeros((POOL_TILE, D_MODEL), _F32)

    @pl.when(i > 0)
    def _():
        _pool_tile(lambda: x_ref[0], False, *weights, out_ref, vbuf)


def _pool_tile(read_h, is_meta_tile, nw_ref, win_ref, wgrp_ref, scale_ref, wout_ref, out_ref, vbuf):
    hn = _rmsnorm(read_h(), nw_ref[...]).astype(_BF16)
    v = _dot(hn, win_ref[:, 0:E_POOL])
    gate = _dot(hn, win_ref[:, E_POOL:2 * E_POOL])
    vbuf[POOL_CARRY:POOL_CARRY + POOL_TILE, :] = v

    parts = []
    for g, w in enumerate(POOL_WINDOWS):
        c0 = g * POOL_GROUP_DIM
        cols = slice(c0, c0 + POOL_GROUP_DIM)

        n_stages = w.bit_length() - 1
        acc = vbuf[POOL_CARRY - SUBLANES * n_stages:POOL_CARRY + POOL_TILE, cols]
        for stage in range(n_stages):
            m = 1 << stage
            acc = acc[SUBLANES:] + acc[SUBLANES - m:acc.shape[0] - m]
        if is_meta_tile:
            pos = lax.broadcasted_iota(jnp.int32, (POOL_TILE, 1), 0) - POOL_PAD
            inv_cnt = 1.0 / jnp.clip(pos + 1, 1, w).astype(_F32)
        else:
            inv_cnt = 1.0 / w
        u = acc * inv_cnt - vbuf[POOL_CARRY:POOL_CARRY + POOL_TILE, cols]
        parts.append(_dot(u.astype(_BF16), wgrp_ref[g]))
    u = jnp.concatenate(parts, axis=-1) * scale_ref[...]
    y = _dot((u * _silu(gate)).astype(_BF16), wout_ref[...])
    out_ref[0] = read_h() + y

    vbuf[0:POOL_CARRY, :] = vbuf[POOL_TILE:POOL_TILE + POOL_CARRY, :]


def _hgrn_layer_kernel(layer, h_ref, nw_ref, win_ref, lbl_ref, onw_ref, wout_ref, fw_ref, out_ref,
                       hn_s, lb_s, q_s, k_s, v_s, g_s, gate_s, qin_s, kin_s, a_s, o_s, st_s, st0_s, growth_s,
                       meta_state):
    row = pl.program_id(0)
    i = pl.program_id(1)
    reuse_meta = (i == 0) & (row > 0)

    @pl.when((i == 0) & (row == 0))
    def _():
        logits = lbl_ref[...]
        e = jnp.exp(logits - jnp.max(logits, axis=0, keepdims=True))
        p = e / jnp.sum(e, axis=0, keepdims=True)
        lb_s[...] = jnp.sum(p[0:layer + 1], axis=0, keepdims=True) - p[0:1]

    @pl.when(reuse_meta)
    def _():
        st_s[...] = meta_state[...]
        st0_s[...] = meta_state[...]

    @pl.when(jnp.logical_not(reuse_meta))
    def _():
        @pl.when(i == 0)
        def _():
            st_s[...] = jnp.zeros(st_s.shape, _F32)
            st0_s[...] = jnp.zeros(st_s.shape, _F32)

        hn = _rmsnorm(h_ref[0], nw_ref[...]).astype(_BF16)
        hn_s[...] = hn
        q_s[...] = _silu(_dot(hn, win_ref[:, 0:E_HGRN]))
        g_s[...] = _dot(hn, win_ref[:, E_HGRN:2 * E_HGRN])

    @pl.when((i > 0) | (row == 0))
    def _():
        _hgrn_mix(h_ref, win_ref, onw_ref, wout_ref, fw_ref, out_ref,
                  hn_s, lb_s, q_s, k_s, v_s, g_s, gate_s, qin_s, kin_s, a_s, o_s, st_s, st0_s, growth_s)

        @pl.when(i == 0)
        def _():
            meta_state[...] = st_s[...]


def _hgrn_mix(h_ref, win_ref, onw_ref, wout_ref, fw_ref, out_ref,
              hn_s, lb_s, q_s, k_s, v_s, g_s, gate_s, qin_s, kin_s, a_s, o_s, st_s, st0_s, growth_s):
    lb = lb_s[...]
    hn = hn_s[...]
    v_s[...] = _dot(hn, win_ref[:, 2 * E_HGRN:3 * E_HGRN]).astype(_BF16)
    gate_s[...] = _silu(_dot(hn, win_ref[:, 3 * E_HGRN:4 * E_HGRN]))

    fp = g_s[...]
    t = jnp.exp2(jnp.abs(fp) * (-LOG2_E))
    r = 1.0 / (1.0 + t)
    tr = t * r
    sig_pos = jnp.where(fp >= 0, r, tr)
    sig_neg = jnp.where(fp >= 0, tr, r)
    g_s[...] = jnp.log2(lb + (1.0 - lb) * sig_pos)
    k_s[...] = (1.0 - lb) * sig_neg

    row = lax.broadcasted_iota(jnp.int32, (CHUNK, CHUNK), 0)
    col = lax.broadcasted_iota(jnp.int32, (CHUNK, CHUNK), 1)
    causal = col <= row
    tri = causal.astype(_BF16)
    tri2 = jnp.concatenate([tri, tri], axis=1)

    d_start_mid, d_mid_end, d_start_end = [], [], []
    for c in range(ROW_TILE // CHUNK):
        rs = slice(c * CHUNK, (c + 1) * CHUNK)
        g = g_s[rs, :]
        g_hi = g.astype(_BF16)
        g_lo = (g - g_hi.astype(_F32)).astype(_BF16)
        b = _dot(tri2, jnp.concatenate([g_hi, g_lo], axis=0))
        b_mid = b[HALF - 1:HALF, :]
        b_end = b[CHUNK - 1:CHUNK, :]
        qin_s[rs, :] = (q_s[rs, :] * jnp.exp2(b - b_mid)).astype(_BF16)
        kin_s[rs, :] = (k_s[rs, :] * jnp.exp2(b_mid - b)).astype(_BF16)
        d_start_mid.append(jnp.exp2(b_mid))
        d_mid_end.append(jnp.exp2(b_end - b_mid))
        d_start_end.append(jnp.exp2(b_end))
        growth_s[c:c + 1, :] = jnp.maximum(b[0:1, :] - b_mid, b_mid - b_end)


    for c in range(ROW_TILE // CHUNK):
        rs = slice(c * CHUNK, (c + 1) * CHUNK)
        for hd in range(N_HEADS):
            ls = slice(hd * HEAD_DIM, (hd + 1) * HEAD_DIM)
            a = _dot_nt(qin_s[rs, ls], kin_s[rs, ls])
            a_s[rs, ls] = jnp.where(causal, a, 0.0).astype(_BF16)

    for c in range(ROW_TILE // CHUNK):
        rs = slice(c * CHUNK, (c + 1) * CHUNK)
        for hd in range(N_HEADS):
            ls = slice(hd * HEAD_DIM, (hd + 1) * HEAD_DIM)
            st = st_s[hd]
            v = v_s[rs, ls]
            k_mid = kin_s[rs, ls]
            s_mid = jnp.transpose(st * d_start_mid[c][:, ls]).astype(_BF16)
            lhs = jnp.concatenate([qin_s[rs, ls], a_s[rs, ls]], axis=1)
            rhs = jnp.concatenate([s_mid, v], axis=0)
            o_s[rs, ls] = _dot(lhs, rhs)
            st_s[hd] = st * d_start_end[c][:, ls] + _dot_tn(v, k_mid) * d_mid_end[c][:, ls]

    _hgrn_output(h_ref, onw_ref, wout_ref, fw_ref, out_ref, gate_s, o_s)

    in_range = jnp.max(growth_s[...]) <= MAX_LOG2_GROWTH

    @pl.when(jnp.logical_not(in_range))
    def _():
        st_s[...] = st0_s[...]
        _hgrn_recurrence_small_steps(q_s, k_s, v_s, g_s, o_s, st_s)
        _hgrn_output(h_ref, onw_ref, wout_ref, fw_ref, out_ref, gate_s, o_s)

    st0_s[...] = st_s[...]


def _hgrn_output(h_ref, onw_ref, wout_ref, fw_ref, out_ref, gate_s, o_s):
    parts = []
    for hd in range(N_HEADS):
        ls = slice(hd * HEAD_DIM, (hd + 1) * HEAD_DIM)
        o = o_s[:, ls]
        parts.append(o * lax.rsqrt(jnp.mean(o * o, axis=-1, keepdims=True) + EPS))
    o = jnp.concatenate(parts, axis=-1) * onw_ref[...]
    y = _dot((o * gate_s[...]).astype(_BF16), wout_ref[...])
    out_ref[0] = _rmsnorm(h_ref[0] + y, fw_ref[...])


def _hgrn_recurrence_small_steps(q_s, k_s, v_s, g_s, o_s, st_s):
    step_row = lax.broadcasted_iota(jnp.int32, (SMALL_STEP, 1), 0)
    r_i = lax.broadcasted_iota(jnp.int32, (SMALL_STEP, SMALL_STEP), 0)
    c_i = lax.broadcasted_iota(jnp.int32, (SMALL_STEP, SMALL_STEP), 1)
    tri = (c_i <= r_i).astype(_BF16)
    tri3 = jnp.concatenate([tri, tri, tri], axis=1)

    def body(j, carry):
        rs = pl.ds(pl.multiple_of(j * SMALL_STEP, SMALL_STEP), SMALL_STEP)
        for hd in range(N_HEADS):
            ls = slice(hd * HEAD_DIM, (hd + 1) * HEAD_DIM)
            g = g_s[rs, ls]
            g_hi = g.astype(_BF16)
            g_md = (g - g_hi.astype(_F32)).astype(_BF16)
            g_lo = (g - g_hi.astype(_F32) - g_md.astype(_F32)).astype(_BF16)
            b = _dot(tri3, jnp.concatenate([g_hi, g_md, g_lo], axis=0))
            b_end = b[SMALL_STEP - 1:SMALL_STEP, :]
            q = q_s[rs, ls]
            k = k_s[rs, ls]
            v = v_s[rs, ls].astype(_F32)
            st = st_s[hd]
            o = _dot_nt((q * jnp.exp2(b)).astype(_BF16), st.astype(_BF16))
            rows = []
            for t in range(SMALL_STEP):
                rel = jnp.where(step_row <= t, b[t:t + 1, :] - b, -jnp.inf)
                score = jnp.sum(q[t:t + 1, :] * k * jnp.exp2(rel), axis=-1, keepdims=True)
                rows.append(jnp.sum(score * v, axis=0, keepdims=True))
            o_s[rs, ls] = o + jnp.concatenate(rows, axis=0)
            k_end = (k * jnp.exp2(b_end - b)).astype(_BF16)
            st_s[hd] = st * jnp.exp2(b_end) + _dot_tn(v.astype(_BF16), k_end)
        return carry

    lax.fori_loop(0, ROW_TILE // SMALL_STEP, body, 0)


def _const_spec(shape):
    zeros = (0,) * len(shape)
    return pl.BlockSpec(shape, lambda b, i: zeros, pipeline_mode=pl.Buffered(1))


def kernel(x, meta_tokens, norm_w, pool_w_in, pool_w_grp, pool_scale, pool_w_out, hgrn_w_in, hgrn_lb_logits,
           hgrn_o_norm, hgrn_w_out, final_norm_w):
    batch, seq, d = x.shape
    depth = norm_w.shape[0]
    assert d == D_MODEL and seq % POOL_TILE == 0 and POOL_TILE % ROW_TILE == 0 and depth == 2
    assert meta_tokens.shape == (N_META, D_MODEL) and N_META >= MAX_WINDOW - 1
    params = pltpu.CompilerParams(dimension_semantics=("arbitrary", "arbitrary"),
                                  vmem_limit_bytes=VMEM_LIMIT_BYTES)

    meta_pad = jnp.concatenate([jnp.zeros((POOL_PAD, D_MODEL), x.dtype), meta_tokens.astype(x.dtype)], axis=0)
    pool_tiles = seq // POOL_TILE + 1

    def tile_spec(rows, first):
        return pl.BlockSpec((1, rows, D_MODEL), lambda b, i: (b, jnp.maximum(i + first, 0), 0))

    h1 = pl.pallas_call(
        _pool_layer_kernel,
        grid=(batch, pool_tiles),
        in_specs=[
            _const_spec((POOL_TILE, D_MODEL)),
            tile_spec(POOL_TILE, -1),
            _const_spec((1, D_MODEL)),
            _const_spec((D_MODEL, 2 * E_POOL)),
            _const_spec((len(POOL_WINDOWS), POOL_GROUP_DIM, POOL_GROUP_DIM)),
            _const_spec((1, E_POOL)),
            _const_spec((E_POOL, D_MODEL)),
        ],
        out_specs=tile_spec(POOL_TILE, 0),
        out_shape=jax.ShapeDtypeStruct((batch, pool_tiles * POOL_TILE, D_MODEL), _F32),
        scratch_shapes=[
            pltpu.VMEM((POOL_TILE + POOL_CARRY, E_POOL), _F32),
            pltpu.VMEM((POOL_CARRY, E_POOL), _F32),
        ],
        compiler_params=params,
        name="pool_layer",
    )(meta_pad, x, norm_w[0:1], pool_w_in[0].astype(_BF16), pool_w_grp[0].astype(_BF16), pool_scale[0:1],
      pool_w_out[0].astype(_BF16))

    out = pl.pallas_call(
        functools.partial(_hgrn_layer_kernel, 1),
        grid=(batch, seq // ROW_TILE + 1),
        in_specs=[
            tile_spec(ROW_TILE, POOL_TILE // ROW_TILE - 1),
            _const_spec((1, D_MODEL)),
            _const_spec((D_MODEL, 4 * E_HGRN)),
            _const_spec((depth, E_HGRN)),
            _const_spec((1, E_HGRN)),
            _const_spec((E_HGRN, D_MODEL)),
            _const_spec((1, D_MODEL)),
        ],
        out_specs=tile_spec(ROW_TILE, -1),
        out_shape=jax.ShapeDtypeStruct((batch, seq, D_MODEL), _F32),
        scratch_shapes=[
            pltpu.VMEM((ROW_TILE, D_MODEL), _BF16),
            pltpu.VMEM((1, E_HGRN), _F32),
            pltpu.VMEM((ROW_TILE, E_HGRN), _F32),
            pltpu.VMEM((ROW_TILE, E_HGRN), _F32),
            pltpu.VMEM((ROW_TILE, E_HGRN), _BF16),
            pltpu.VMEM((ROW_TILE, E_HGRN), _F32),
            pltpu.VMEM((ROW_TILE, E_HGRN), _F32),
            pltpu.VMEM((ROW_TILE, E_HGRN), _BF16),
            pltpu.VMEM((ROW_TILE, E_HGRN), _BF16),
            pltpu.VMEM((ROW_TILE, E_HGRN), _BF16),
            pltpu.VMEM((ROW_TILE, E_HGRN), _F32),
            pltpu.VMEM((N_HEADS, HEAD_DIM, HEAD_DIM), _F32),
            pltpu.VMEM((N_HEADS, HEAD_DIM, HEAD_DIM), _F32),
            pltpu.VMEM((ROW_TILE // CHUNK, E_HGRN), _F32),
            pltpu.VMEM((N_HEADS, HEAD_DIM, HEAD_DIM), _F32),
        ],
        compiler_params=params,
        name="hgrn_layer",
    )(h1, norm_w[1:2], hgrn_w_in[0].astype(_BF16), hgrn_lb_logits, hgrn_o_norm[0:1], hgrn_w_out[0].astype(_BF16),
      final_norm_w.reshape(1, D_MODEL))
    return out
```

```python
import functools

import jax
import jax.numpy as jnp
from jax import lax
from jax.experimental import pallas as pl
from jax.experimental.pallas import tpu as pltpu

D_MODEL = 1024
N_META = 16
E_POOL = 2048
POOL_WINDOWS = (2, 4, 8, 16)
POOL_GROUP_DIM = E_POOL // len(POOL_WINDOWS)
MAX_WINDOW = max(POOL_WINDOWS)
SUBLANES = 8
POOL_CARRY = SUBLANES * (MAX_WINDOW.bit_length() - 1)
E_HGRN = 2048
HEAD_DIM = 128
N_HEADS = E_HGRN // HEAD_DIM
EPS = 1e-6
LOG2_E = 1.4426950408889634

POOL_TILE = 512
POOL_PAD = POOL_TILE - N_META
ROW_TILE = 256
CHUNK = 128
HALF = CHUNK // 2
MAX_LOG2_GROWTH = 100.0
SMALL_STEP = 16
VMEM_LIMIT_BYTES = 56 * 1024 * 1024

_F32 = jnp.float32
_BF16 = jnp.bfloat16


def _rmsnorm(x, w):
    ms = jnp.mean(x * x, axis=-1, keepdims=True)
    return x * lax.rsqrt(ms + EPS) * w


def _silu(x):
    half = 0.5 * x
    return half + half * jnp.tanh(half)


def _dot(a, b):
    return jnp.dot(a, b, preferred_element_type=_F32)


def _dot_nt(a, b):
    return lax.dot_general(a, b, (((1,), (1,)), ((), ())), preferred_element_type=_F32)


def _dot_tn(a, b):
    return lax.dot_general(a, b, (((0,), (0,)), ((), ())), preferred_element_type=_F32)


def _pool_layer_kernel(meta_ref, x_ref, nw_ref, win_ref, wgrp_ref, scale_ref, wout_ref, out_ref, vbuf, meta_carry):
    row = pl.program_id(0)
    i = pl.program_id(1)
    weights = (nw_ref, win_ref, wgrp_ref, scale_ref, wout_ref)

    @pl.when((i == 0) & (row == 0))
    def _():
        vbuf[0:POOL_CARRY, :] = jnp.zeros((POOL_CARRY, E_POOL), _F32)
        _pool_tile(lambda: meta_ref[...], True, *weights, out_ref, vbuf)
        meta_carry[...] = vbuf[0:POOL_CARRY, :]

    @pl.when((i == 0) & (row > 0))
    def _():
        vbuf[0:POOL_CARRY, :] = meta_carry[...]
        out_ref[0] = jnp.zeros((POOL_TILE, D_MODEL), _F32)

    @pl.when(i > 0)
    def _():
        _pool_tile(lambda: x_ref[0], False, *weights, out_ref, vbuf)


def _pool_tile(read_h, is_meta_tile, nw_ref, win_ref, wgrp_ref, scale_ref, wout_ref, out_ref, vbuf):
    hn = _rmsnorm(read_h(), nw_ref[...]).astype(_BF16)
    v = _dot(hn, win_ref[:, 0:E_POOL])
    gate = _dot(hn, win_ref[:, E_POOL:2 * E_POOL])
    vbuf[POOL_CARRY:POOL_CARRY + POOL_TILE, :] = v

    parts = []
    for g, w in enumerate(POOL_WINDOWS):
        c0 = g * POOL_GROUP_DIM
        cols = slice(c0, c0 + POOL_GROUP_DIM)

        n_stages = w.bit_length() - 1
        acc = vbuf[POOL_CARRY - SUBLANES * n_stages:POOL_CARRY + POOL_TILE, cols]
        for stage in range(n_stages):
            m = 1 << stage
            acc = acc[SUBLANES:] + acc[SUBLANES - m:acc.shape[0] - m]
        if is_meta_tile:
            pos = lax.broadcasted_iota(jnp.int32, (POOL_TILE, 1), 0) - POOL_PAD
            inv_cnt = 1.0 / jnp.clip(pos + 1, 1, w).astype(_F32)
        else:
            inv_cnt = 1.0 / w
        u = acc * inv_cnt - vbuf[POOL_CARRY:POOL_CARRY + POOL_TILE, cols]
        parts.append(_dot(u.astype(_BF16), wgrp_ref[g]))
    u = jnp.concatenate(parts, axis=-1) * scale_ref[...]
    y = _dot((u * _silu(gate)).astype(_BF16), wout_ref[...])
    out_ref[0] = read_h() + y

    vbuf[0:POOL_CARRY, :] = vbuf[POOL_TILE:POOL_TILE + POOL_CARRY, :]


def _hgrn_prologue(layer, lbl_ref, lb_s):
    logits = lbl_ref[...]
    e = jnp.exp(logits - jnp.max(logits, axis=0, keepdims=True))
    p = e / jnp.sum(e, axis=0, keepdims=True)
    lb_s[...] = jnp.sum(p[0:layer + 1], axis=0, keepdims=True) - p[0:1]


def _hgrn_project(h_ref, nw_ref, win_ref, hn_s, q_s, g_s):
    hn = _rmsnorm(h_ref[0], nw_ref[...]).astype(_BF16)
    hn_s[...] = hn
    q_s[...] = _silu(_dot(hn, win_ref[:, 0:E_HGRN]))
    g_s[...] = _dot(hn, win_ref[:, E_HGRN:2 * E_HGRN])


def _hgrn_fast_kernel(layer, h_ref, nw_ref, win_ref, lbl_ref, onw_ref, wout_ref, fw_ref, out_ref, growth_ref,
                      hn_s, lb_s, q_s, k_s, v_s, g_s, gate_s, qin_s, kin_s, a_s, o_s, st_s, meta_state):
    row = pl.program_id(0)
    i = pl.program_id(1)
    reuse_meta = (i == 0) & (row > 0)

    @pl.when((i == 0) & (row == 0))
    def _():
        _hgrn_prologue(layer, lbl_ref, lb_s)

    @pl.when(reuse_meta)
    def _():
        st_s[...] = meta_state[...]
        growth_ref[...] = jnp.zeros(growth_ref.shape, _F32)

    @pl.when(jnp.logical_not(reuse_meta))
    def _():
        @pl.when(i == 0)
        def _():
            st_s[...] = jnp.zeros(st_s.shape, _F32)

        def store_growth(c, value):
            growth_ref[0, 0, c:c + 1, :] = value

        _hgrn_project(h_ref, nw_ref, win_ref, hn_s, q_s, g_s)
        _hgrn_mix(win_ref, hn_s, lb_s, q_s, k_s, v_s, g_s, gate_s, qin_s, kin_s, a_s, o_s, st_s, store_growth)
        _hgrn_output(h_ref, onw_ref, wout_ref, fw_ref, out_ref, gate_s, o_s)

        @pl.when(i == 0)
        def _():
            meta_state[...] = st_s[...]


def _hgrn_safe_kernel(layer, h_ref, nw_ref, win_ref, lbl_ref, onw_ref, wout_ref, fw_ref, out_ref,
                      hn_s, lb_s, q_s, k_s, v_s, g_s, gate_s, qin_s, kin_s, a_s, o_s, st_s, st0_s, growth_s,
                      meta_state):
    row = pl.program_id(0)
    i = pl.program_id(1)
    reuse_meta = (i == 0) & (row > 0)

    @pl.when((i == 0) & (row == 0))
    def _():
        _hgrn_prologue(layer, lbl_ref, lb_s)

    @pl.when(reuse_meta)
    def _():
        st_s[...] = meta_state[...]
        st0_s[...] = meta_state[...]

    @pl.when(jnp.logical_not(reuse_meta))
    def _():
        @pl.when(i == 0)
        def _():
            st_s[...] = jnp.zeros(st_s.shape, _F32)
            st0_s[...] = jnp.zeros(st_s.shape, _F32)

        _hgrn_project(h_ref, nw_ref, win_ref, hn_s, q_s, g_s)

    @pl.when((i > 0) | (row == 0))
    def _():
        def store_growth(c, value):
            growth_s[c:c + 1, :] = value

        _hgrn_mix(win_ref, hn_s, lb_s, q_s, k_s, v_s, g_s, gate_s, qin_s, kin_s, a_s, o_s, st_s, store_growth)
        _hgrn_output(h_ref, onw_ref, wout_ref, fw_ref, out_ref, gate_s, o_s)

        in_range = jnp.max(growth_s[...]) <= MAX_LOG2_GROWTH

        @pl.when(jnp.logical_not(in_range))
        def _():
            st_s[...] = st0_s[...]
            _hgrn_recurrence_small_steps(q_s, k_s, v_s, g_s, o_s, st_s)
            _hgrn_output(h_ref, onw_ref, wout_ref, fw_ref, out_ref, gate_s, o_s)

        st0_s[...] = st_s[...]

        @pl.when(i == 0)
        def _():
            meta_state[...] = st_s[...]


def _hgrn_mix(win_ref, hn_s, lb_s, q_s, k_s, v_s, g_s, gate_s, qin_s, kin_s, a_s, o_s, st_s, store_growth):
    lb = lb_s[...]
    hn = hn_s[...]
    v_s[...] = _dot(hn, win_ref[:, 2 * E_HGRN:3 * E_HGRN]).astype(_BF16)
    gate_s[...] = _silu(_dot(hn, win_ref[:, 3 * E_HGRN:4 * E_HGRN]))

    fp = g_s[...]
    t = jnp.exp2(jnp.abs(fp) * (-LOG2_E))
    r = 1.0 / (1.0 + t)
    tr = t * r
    sig_pos = jnp.where(fp >= 0, r, tr)
    sig_neg = jnp.where(fp >= 0, tr, r)
    g_s[...] = jnp.log2(lb + (1.0 - lb) * sig_pos)
    k_s[...] = (1.0 - lb) * sig_neg

    row = lax.broadcasted_iota(jnp.int32, (CHUNK, CHUNK), 0)
    col = lax.broadcasted_iota(jnp.int32, (CHUNK, CHUNK), 1)
    causal = col <= row
    tri = causal.astype(_BF16)
    tri2 = jnp.concatenate([tri, tri], axis=1)

    d_start_mid, d_mid_end, d_start_end = [], [], []
    for c in range(ROW_TILE // CHUNK):
        rs = slice(c * CHUNK, (c + 1) * CHUNK)
        g = g_s[rs, :]
        g_hi = g.astype(_BF16)
        g_lo = (g - g_hi.astype(_F32)).astype(_BF16)
        b = _dot(tri2, jnp.concatenate([g_hi, g_lo], axis=0))
        b_mid = b[HALF - 1:HALF, :]
        b_end = b[CHUNK - 1:CHUNK, :]
        qin_s[rs, :] = (q_s[rs, :] * jnp.exp2(b - b_mid)).astype(_BF16)
        kin_s[rs, :] = (k_s[rs, :] * jnp.exp2(b_mid - b)).astype(_BF16)
        d_start_mid.append(jnp.exp2(b_mid))
        d_mid_end.append(jnp.exp2(b_end - b_mid))
        d_start_end.append(jnp.exp2(b_end))
        store_growth(c, jnp.maximum(b[0:1, :] - b_mid, b_mid - b_end))

    for c in range(ROW_TILE // CHUNK):
        rs = slice(c * CHUNK, (c + 1) * CHUNK)
        for hd in range(N_HEADS):
            ls = slice(hd * HEAD_DIM, (hd + 1) * HEAD_DIM)
            a = _dot_nt(qin_s[rs, ls], kin_s[rs, ls])
            a_s[rs, ls] = jnp.where(causal, a, 0.0).astype(_BF16)

    for c in range(ROW_TILE // CHUNK):
        rs = slice(c * CHUNK, (c + 1) * CHUNK)
        for hd in range(N_HEADS):
            ls = slice(hd * HEAD_DIM, (hd + 1) * HEAD_DIM)
            st = st_s[hd]
            v = v_s[rs, ls]
            k_mid = kin_s[rs, ls]
            s_mid = jnp.transpose(st * d_start_mid[c][:, ls]).astype(_BF16)
            lhs = jnp.concatenate([qin_s[rs, ls], a_s[rs, ls]], axis=1)
            rhs = jnp.concatenate([s_mid, v], axis=0)
            o_s[rs, ls] = _dot(lhs, rhs)
            st_s[hd] = st * d_start_end[c][:, ls] + _dot_tn(v, k_mid) * d_mid_end[c][:, ls]


def _hgrn_output(h_ref, onw_ref, wout_ref, fw_ref, out_ref, gate_s, o_s):
    parts = []
    for hd in range(N_HEADS):
        ls = slice(hd * HEAD_DIM, (hd + 1) * HEAD_DIM)
        o = o_s[:, ls]
        parts.append(o * lax.rsqrt(jnp.mean(o * o, axis=-1, keepdims=True) + EPS))
    o = jnp.concatenate(parts, axis=-1) * onw_ref[...]
    y = _dot((o * gate_s[...]).astype(_BF16), wout_ref[...])
    out_ref[0] = _rmsnorm(h_ref[0] + y, fw_ref[...])


def _hgrn_recurrence_small_steps(q_s, k_s, v_s, g_s, o_s, st_s):
    step_row = lax.broadcasted_iota(jnp.int32, (SMALL_STEP, 1), 0)
    r_i = lax.broadcasted_iota(jnp.int32, (SMALL_STEP, SMALL_STEP), 0)
    c_i = lax.broadcasted_iota(jnp.int32, (SMALL_STEP, SMALL_STEP), 1)
    tri = (c_i <= r_i).astype(_BF16)
    tri3 = jnp.concatenate([tri, tri, tri], axis=1)

    def body(j, carry):
        rs = pl.ds(pl.multiple_of(j * SMALL_STEP, SMALL_STEP), SMALL_STEP)
        for hd in range(N_HEADS):
            ls = slice(hd * HEAD_DIM, (hd + 1) * HEAD_DIM)
            g = g_s[rs, ls]
            g_hi = g.astype(_BF16)
            g_md = (g - g_hi.astype(_F32)).astype(_BF16)
            g_lo = (g - g_hi.astype(_F32) - g_md.astype(_F32)).astype(_BF16)
            b = _dot(tri3, jnp.concatenate([g_hi, g_md, g_lo], axis=0))
            b_end = b[SMALL_STEP - 1:SMALL_STEP, :]
            q = q_s[rs, ls]
            k = k_s[rs, ls]
            v = v_s[rs, ls].astype(_F32)
            st = st_s[hd]
            o = _dot_nt((q * jnp.exp2(b)).astype(_BF16), st.astype(_BF16))
            rows = []
            for t in range(SMALL_STEP):
                rel = jnp.where(step_row <= t, b[t:t + 1, :] - b, -jnp.inf)
                score = jnp.sum(q[t:t + 1, :] * k * jnp.exp2(rel), axis=-1, keepdims=True)
                rows.append(jnp.sum(score * v, axis=0, keepdims=True))
            o_s[rs, ls] = o + jnp.concatenate(rows, axis=0)
            k_end = (k * jnp.exp2(b_end - b)).astype(_BF16)
            st_s[hd] = st * jnp.exp2(b_end) + _dot_tn(v.astype(_BF16), k_end)
        return carry

    lax.fori_loop(0, ROW_TILE // SMALL_STEP, body, 0)


def _const_spec(shape):
    zeros = (0,) * len(shape)
    return pl.BlockSpec(shape, lambda b, i: zeros, pipeline_mode=pl.Buffered(1))


def kernel(x, meta_tokens, norm_w, pool_w_in, pool_w_grp, pool_scale, pool_w_out, hgrn_w_in, hgrn_lb_logits,
           hgrn_o_norm, hgrn_w_out, final_norm_w):
    batch, seq, d = x.shape
    depth = norm_w.shape[0]
    assert d == D_MODEL and seq % POOL_TILE == 0 and POOL_TILE % ROW_TILE == 0 and depth == 2
    assert meta_tokens.shape == (N_META, D_MODEL) and N_META >= MAX_WINDOW - 1
    params = pltpu.CompilerParams(dimension_semantics=("arbitrary", "arbitrary"),
                                  vmem_limit_bytes=VMEM_LIMIT_BYTES)

    meta_pad = jnp.concatenate([jnp.zeros((POOL_PAD, D_MODEL), x.dtype), meta_tokens.astype(x.dtype)], axis=0)
    pool_tiles = seq // POOL_TILE + 1

    def tile_spec(rows, first):
        return pl.BlockSpec((1, rows, D_MODEL), lambda b, i: (b, jnp.maximum(i + first, 0), 0))

    h1 = pl.pallas_call(
        _pool_layer_kernel,
        grid=(batch, pool_tiles),
        in_specs=[
            _const_spec((POOL_TILE, D_MODEL)),
            tile_spec(POOL_TILE, -1),
            _const_spec((1, D_MODEL)),
            _const_spec((D_MODEL, 2 * E_POOL)),
            _const_spec((len(POOL_WINDOWS), POOL_GROUP_DIM, POOL_GROUP_DIM)),
            _const_spec((1, E_POOL)),
            _const_spec((E_POOL, D_MODEL)),
        ],
        out_specs=tile_spec(POOL_TILE, 0),
        out_shape=jax.ShapeDtypeStruct((batch, pool_tiles * POOL_TILE, D_MODEL), _F32),
        scratch_shapes=[
            pltpu.VMEM((POOL_TILE + POOL_CARRY, E_POOL), _F32),
            pltpu.VMEM((POOL_CARRY, E_POOL), _F32),
        ],
        compiler_params=params,
        name="pool_layer",
    )(meta_pad, x, norm_w[0:1], pool_w_in[0].astype(_BF16), pool_w_grp[0].astype(_BF16), pool_scale[0:1],
      pool_w_out[0].astype(_BF16))

    hgrn_tiles = seq // ROW_TILE + 1
    n_chunks = ROW_TILE // CHUNK
    tile_f32 = pltpu.VMEM((ROW_TILE, E_HGRN), _F32)
    tile_bf16 = pltpu.VMEM((ROW_TILE, E_HGRN), _BF16)
    state = pltpu.VMEM((N_HEADS, HEAD_DIM, HEAD_DIM), _F32)
    hgrn_operands = (h1, norm_w[1:2], hgrn_w_in[0].astype(_BF16), hgrn_lb_logits, hgrn_o_norm[0:1],
                     hgrn_w_out[0].astype(_BF16), final_norm_w.reshape(1, D_MODEL))
    hgrn_in_specs = [
        tile_spec(ROW_TILE, POOL_TILE // ROW_TILE - 1),
        _const_spec((1, D_MODEL)),
        _const_spec((D_MODEL, 4 * E_HGRN)),
        _const_spec((depth, E_HGRN)),
        _const_spec((1, E_HGRN)),
        _const_spec((E_HGRN, D_MODEL)),
        _const_spec((1, D_MODEL)),
    ]
    hgrn_scratch = [
        pltpu.VMEM((ROW_TILE, D_MODEL), _BF16),
        pltpu.VMEM((1, E_HGRN), _F32),
        tile_f32,
        tile_f32,
        tile_bf16,
        tile_f32,
        tile_f32,
        tile_bf16,
        tile_bf16,
        tile_bf16,
        tile_f32,
        state,
    ]
    out_struct = jax.ShapeDtypeStruct((batch, seq, D_MODEL), _F32)

    out_fast, growth = pl.pallas_call(
        functools.partial(_hgrn_fast_kernel, 1),
        grid=(batch, hgrn_tiles),
        in_specs=hgrn_in_specs,
        out_specs=[tile_spec(ROW_TILE, -1),
                   pl.BlockSpec((1, 1, n_chunks, E_HGRN), lambda b, i: (b, i, 0, 0))],
        out_shape=[out_struct, jax.ShapeDtypeStruct((batch, hgrn_tiles, n_chunks, E_HGRN), _F32)],
        scratch_shapes=hgrn_scratch + [state],
        compiler_params=params,
        name="hgrn_layer",
    )(*hgrn_operands)

    def hgrn_safe():
        return pl.pallas_call(
            functools.partial(_hgrn_safe_kernel, 1),
            grid=(batch, hgrn_tiles),
            in_specs=hgrn_in_specs,
            out_specs=tile_spec(ROW_TILE, -1),
            out_shape=out_struct,
            scratch_shapes=hgrn_scratch + [
                state,
                pltpu.VMEM((n_chunks, E_HGRN), _F32),
                state,
            ],
            compiler_params=params,
            name="hgrn_layer_any_range",
        )(*hgrn_operands)

    in_range = jnp.max(growth) <= MAX_LOG2_GROWTH
    return lax.cond(in_range, lambda: out_fast, hgrn_safe)
```

```python
import functools

import jax
import jax.numpy as jnp
from jax import lax
from jax.experimental import pallas as pl
from jax.experimental.pallas import tpu as pltpu

D_MODEL = 1024
N_META = 16
E_POOL = 2048
POOL_WINDOWS = (2, 4, 8, 16)
POOL_GROUP_DIM = E_POOL // len(POOL_WINDOWS)
MAX_WINDOW = max(POOL_WINDOWS)
SUBLANES = 8
POOL_CARRY = SUBLANES * (MAX_WINDOW.bit_length() - 1)
E_HGRN = 2048
HEAD_DIM = 128
N_HEADS = E_HGRN // HEAD_DIM
EPS = 1e-6
LOG2_E = 1.4426950408889634

POOL_TILE = 512
POOL_PAD = POOL_TILE - N_META
ROW_TILE = 256
CHUNK = 128
HALF = CHUNK // 2
MAX_LOG2_GROWTH = 100.0
SMALL_STEP = 16
VMEM_LIMIT_BYTES = 56 * 1024 * 1024

_F32 = jnp.float32
_BF16 = jnp.bfloat16


def _rmsnorm(x, w):
    ms = jnp.mean(x * x, axis=-1, keepdims=True)
    return x * lax.rsqrt(ms + EPS) * w


def _silu(x):
    half = 0.5 * x
    return half + half * jnp.tanh(half)


def _dot(a, b):
    return jnp.dot(a, b, preferred_element_type=_F32)


def _dot_nt(a, b):
    return lax.dot_general(a, b, (((1,), (1,)), ((), ())), preferred_element_type=_F32)


def _dot_tn(a, b):
    return lax.dot_general(a, b, (((0,), (0,)), ((), ())), preferred_element_type=_F32)


def _pool_layer_kernel(meta_ref, x_ref, nw_ref, win_ref, wgrp_ref, scale_ref, wout_ref, out_ref, vbuf, meta_carry):
    row = pl.program_id(0)
    i = pl.program_id(1)
    weights = (nw_ref, win_ref, wgrp_ref, scale_ref, wout_ref)

    @pl.when((i == 0) & (row == 0))
    def _():
        vbuf[0:POOL_CARRY, :] = jnp.zeros((POOL_CARRY, E_POOL), _F32)
        _pool_tile(lambda: meta_ref[...], True, *weights, out_ref, vbuf)
        meta_carry[...] = vbuf[0:POOL_CARRY, :]

    @pl.when((i == 0) & (row > 0))
    def _():
        vbuf[0:POOL_CARRY, :] = meta_carry[...]
        out_ref[0] = jnp.zeros((POOL_TILE, D_MODEL), _F32)

    @pl.when(i > 0)
    def _():
        _pool_tile(lambda: x_ref[0], False, *weights, out_ref, vbuf)


def _pool_tile(read_h, is_meta_tile, nw_ref, win_ref, wgrp_ref, scale_ref, wout_ref, out_ref, vbuf):
    hn = _rmsnorm(read_h(), nw_ref[...]).astype(_BF16)
    v = _dot(hn, win_ref[:, 0:E_POOL])
    gate = _dot(hn, win_ref[:, E_POOL:2 * E_POOL])
    vbuf[POOL_CARRY:POOL_CARRY + POOL_TILE, :] = v

    parts = []
    for g, w in enumerate(POOL_WINDOWS):
        c0 = g * POOL_GROUP_DIM
        cols = slice(c0, c0 + POOL_GROUP_DIM)

        n_stages = w.bit_length() - 1
        acc = vbuf[POOL_CARRY - SUBLANES * n_stages:POOL_CARRY + POOL_TILE, cols]
        for stage in range(n_stages):
            m = 1 << stage
            acc = acc[SUBLANES:] + acc[SUBLANES - m:acc.shape[0] - m]
        if is_meta_tile:
            pos = lax.broadcasted_iota(jnp.int32, (POOL_TILE, 1), 0) - POOL_PAD
            inv_cnt = 1.0 / jnp.clip(pos + 1, 1, w).astype(_F32)
        else:
            inv_cnt = 1.0 / w
        u = acc * inv_cnt - vbuf[POOL_CARRY:POOL_CARRY + POOL_TILE, cols]
        parts.append(_dot(u.astype(_BF16), wgrp_ref[g]))
    u = jnp.concatenate(parts, axis=-1) * scale_ref[...]
    y = _dot((u * _silu(gate)).astype(_BF16), wout_ref[...])
    out_ref[0] = read_h() + y

    vbuf[0:POOL_CARRY, :] = vbuf[POOL_TILE:POOL_TILE + POOL_CARRY, :]


def _hgrn_prologue(layer, lbl_ref, lb_s):
    logits = lbl_ref[...]
    e = jnp.exp(logits - jnp.max(logits, axis=0, keepdims=True))
    p = e / jnp.sum(e, axis=0, keepdims=True)
    lb_s[...] = jnp.sum(p[0:layer + 1], axis=0, keepdims=True) - p[0:1]


def _hgrn_project(h_ref, nw_ref, win_ref, hn_s, q_s, g_s):
    hn = _rmsnorm(h_ref[0], nw_ref[...]).astype(_BF16)
    hn_s[...] = hn
    q_s[...] = _silu(_dot(hn, win_ref[:, 0:E_HGRN]).astype(_BF16))
    g_s[...] = _dot(hn, win_ref[:, E_HGRN:2 * E_HGRN])


def _hgrn_fast_kernel(layer, h_ref, nw_ref, win_ref, lbl_ref, onw_ref, wout_ref, fw_ref, out_ref, growth_ref,
                      hn_s, lb_s, q_s, k_s, v_s, g_s, gate_s, qin_s, kin_s, a_s, o_s, st_s, meta_state):
    row = pl.program_id(0)
    i = pl.program_id(1)
    reuse_meta = (i == 0) & (row > 0)

    @pl.when((i == 0) & (row == 0))
    def _():
        _hgrn_prologue(layer, lbl_ref, lb_s)

    @pl.when(reuse_meta)
    def _():
        st_s[...] = meta_state[...]
        growth_ref[...] = jnp.zeros(growth_ref.shape, _F32)

    @pl.when(jnp.logical_not(reuse_meta))
    def _():
        @pl.when(i == 0)
        def _():
            st_s[...] = jnp.zeros(st_s.shape, _F32)

        def store_growth(c, value):
            growth_ref[0, 0, c:c + 1, :] = value

        _hgrn_project(h_ref, nw_ref, win_ref, hn_s, q_s, g_s)
        _hgrn_mix(win_ref, hn_s, lb_s, q_s, k_s, v_s, g_s, gate_s, qin_s, kin_s, a_s, o_s, st_s, store_growth)
        _hgrn_output(h_ref, onw_ref, wout_ref, fw_ref, out_ref, gate_s, o_s)

        @pl.when(i == 0)
        def _():
            meta_state[...] = st_s[...]


def _hgrn_safe_kernel(layer, h_ref, nw_ref, win_ref, lbl_ref, onw_ref, wout_ref, fw_ref, out_ref,
                      hn_s, lb_s, q_s, k_s, v_s, g_s, gate_s, qin_s, kin_s, a_s, o_s, st_s, st0_s, growth_s,
                      meta_state):
    row = pl.program_id(0)
    i = pl.program_id(1)
    reuse_meta = (i == 0) & (row > 0)

    @pl.when((i == 0) & (row == 0))
    def _():
        _hgrn_prologue(layer, lbl_ref, lb_s)

    @pl.when(reuse_meta)
    def _():
        st_s[...] = meta_state[...]
        st0_s[...] = meta_state[...]

    @pl.when(jnp.logical_not(reuse_meta))
    def _():
        @pl.when(i == 0)
        def _():
            st_s[...] = jnp.zeros(st_s.shape, _F32)
            st0_s[...] = jnp.zeros(st_s.shape, _F32)

        _hgrn_project(h_ref, nw_ref, win_ref, hn_s, q_s, g_s)

    @pl.when((i > 0) | (row == 0))
    def _():
        def store_growth(c, value):
            growth_s[c:c + 1, :] = value

        _hgrn_mix(win_ref, hn_s, lb_s, q_s, k_s, v_s, g_s, gate_s, qin_s, kin_s, a_s, o_s, st_s, store_growth)
        _hgrn_output(h_ref, onw_ref, wout_ref, fw_ref, out_ref, gate_s, o_s)

        in_range = jnp.max(growth_s[...]) <= MAX_LOG2_GROWTH

        @pl.when(jnp.logical_not(in_range))
        def _():
            st_s[...] = st0_s[...]
            _hgrn_recurrence_small_steps(q_s, k_s, v_s, g_s, o_s, st_s)
            _hgrn_output(h_ref, onw_ref, wout_ref, fw_ref, out_ref, gate_s, o_s)

        st0_s[...] = st_s[...]

        @pl.when(i == 0)
        def _():
            meta_state[...] = st_s[...]


def _hgrn_mix(win_ref, hn_s, lb_s, q_s, k_s, v_s, g_s, gate_s, qin_s, kin_s, a_s, o_s, st_s, store_growth):
    lb = lb_s[...]
    hn = hn_s[...]
    v_s[...] = _dot(hn, win_ref[:, 2 * E_HGRN:3 * E_HGRN]).astype(_BF16)
    gate_s[...] = _silu(_dot(hn, win_ref[:, 3 * E_HGRN:4 * E_HGRN]).astype(_BF16))

    fp = g_s[...]
    t = jnp.exp2(jnp.abs(fp) * (-LOG2_E))
    r = 1.0 / (1.0 + t)
    tr = t * r
    sig_pos = jnp.where(fp >= 0, r, tr)
    sig_neg = jnp.where(fp >= 0, tr, r)
    g_s[...] = jnp.log2(lb + (1.0 - lb) * sig_pos)
    k_s[...] = (1.0 - lb) * sig_neg

    row = lax.broadcasted_iota(jnp.int32, (CHUNK, CHUNK), 0)
    col = lax.broadcasted_iota(jnp.int32, (CHUNK, CHUNK), 1)
    causal = col <= row
    tri = causal.astype(_BF16)
    tri2 = jnp.concatenate([tri, tri], axis=1)

    d_start_mid, d_mid_end, d_start_end = [], [], []
    for c in range(ROW_TILE // CHUNK):
        rs = slice(c * CHUNK, (c + 1) * CHUNK)
        g = g_s[rs, :]
        g_hi = g.astype(_BF16)
        g_lo = (g - g_hi.astype(_F32)).astype(_BF16)
        b = _dot(tri2, jnp.concatenate([g_hi, g_lo], axis=0))
        b_mid = b[HALF - 1:HALF, :]
        b_end = b[CHUNK - 1:CHUNK, :]
        qin_s[rs, :] = q_s[rs, :] * jnp.exp2(b - b_mid).astype(_BF16)
        kin_s[rs, :] = (k_s[rs, :] * jnp.exp2(b_mid - b)).astype(_BF16)
        d_start_mid.append(jnp.exp2(b_mid))
        d_mid_end.append(jnp.exp2(b_end - b_mid))
        d_start_end.append(jnp.exp2(b_end))
        store_growth(c, jnp.maximum(b[0:1, :] - b_mid, b_mid - b_end))

    for c in range(ROW_TILE // CHUNK):
        rs = slice(c * CHUNK, (c + 1) * CHUNK)
        for hd in range(N_HEADS):
            ls = slice(hd * HEAD_DIM, (hd + 1) * HEAD_DIM)
            a = _dot_nt(qin_s[rs, ls], kin_s[rs, ls])
            a_s[rs, ls] = jnp.where(causal, a, 0.0).astype(_BF16)

    for c in range(ROW_TILE // CHUNK):
        rs = slice(c * CHUNK, (c + 1) * CHUNK)
        for hd in range(N_HEADS):
            ls = slice(hd * HEAD_DIM, (hd + 1) * HEAD_DIM)
            st = st_s[hd]
            v = v_s[rs, ls]
            k_mid = kin_s[rs, ls]
            s_mid = jnp.transpose(st * d_start_mid[c][:, ls]).astype(_BF16)
            lhs = jnp.concatenate([qin_s[rs, ls], a_s[rs, ls]], axis=1)
            rhs = jnp.concatenate([s_mid, v], axis=0)
            o_s[rs, ls] = _dot(lhs, rhs)
            st_s[hd] = st * d_start_end[c][:, ls] + _dot_tn(v, k_mid) * d_mid_end[c][:, ls]


def _hgrn_output(h_ref, onw_ref, wout_ref, fw_ref, out_ref, gate_s, o_s):
    parts = []
    for hd in range(N_HEADS):
        ls = slice(hd * HEAD_DIM, (hd + 1) * HEAD_DIM)
        o = o_s[:, ls]
        parts.append(o * lax.rsqrt(jnp.mean(o * o, axis=-1, keepdims=True) + EPS))
    o = jnp.concatenate(parts, axis=-1) * onw_ref[...]
    y = _dot(o.astype(_BF16) * gate_s[...], wout_ref[...])
    out_ref[0] = _rmsnorm(h_ref[0] + y, fw_ref[...])


def _hgrn_recurrence_small_steps(q_s, k_s, v_s, g_s, o_s, st_s):
    step_row = lax.broadcasted_iota(jnp.int32, (SMALL_STEP, 1), 0)
    r_i = lax.broadcasted_iota(jnp.int32, (SMALL_STEP, SMALL_STEP), 0)
    c_i = lax.broadcasted_iota(jnp.int32, (SMALL_STEP, SMALL_STEP), 1)
    tri = (c_i <= r_i).astype(_BF16)
    tri3 = jnp.concatenate([tri, tri, tri], axis=1)

    def body(j, carry):
        rs = pl.ds(pl.multiple_of(j * SMALL_STEP, SMALL_STEP), SMALL_STEP)
        for hd in range(N_HEADS):
            ls = slice(hd * HEAD_DIM, (hd + 1) * HEAD_DIM)
            g = g_s[rs, ls]
            g_hi = g.astype(_BF16)
            g_md = (g - g_hi.astype(_F32)).astype(_BF16)
            g_lo = (g - g_hi.astype(_F32) - g_md.astype(_F32)).astype(_BF16)
            b = _dot(tri3, jnp.concatenate([g_hi, g_md, g_lo], axis=0))
            b_end = b[SMALL_STEP - 1:SMALL_STEP, :]
            q = q_s[rs, ls].astype(_F32)
            k = k_s[rs, ls]
            v = v_s[rs, ls].astype(_F32)
            st = st_s[hd]
            o = _dot_nt((q * jnp.exp2(b)).astype(_BF16), st.astype(_BF16))
            rows = []
            for t in range(SMALL_STEP):
                rel = jnp.where(step_row <= t, b[t:t + 1, :] - b, -jnp.inf)
                score = jnp.sum(q[t:t + 1, :] * k * jnp.exp2(rel), axis=-1, keepdims=True)
                rows.append(jnp.sum(score * v, axis=0, keepdims=True))
            o_s[rs, ls] = o + jnp.concatenate(rows, axis=0)
            k_end = (k * jnp.exp2(b_end - b)).astype(_BF16)
            st_s[hd] = st * jnp.exp2(b_end) + _dot_tn(v.astype(_BF16), k_end)
        return carry

    lax.fori_loop(0, ROW_TILE // SMALL_STEP, body, 0)


def _const_spec(shape):
    zeros = (0,) * len(shape)
    return pl.BlockSpec(shape, lambda b, i: zeros, pipeline_mode=pl.Buffered(1))


def kernel(x, meta_tokens, norm_w, pool_w_in, pool_w_grp, pool_scale, pool_w_out, hgrn_w_in, hgrn_lb_logits,
           hgrn_o_norm, hgrn_w_out, final_norm_w):
    batch, seq, d = x.shape
    depth = norm_w.shape[0]
    assert d == D_MODEL and seq % POOL_TILE == 0 and POOL_TILE % ROW_TILE == 0 and depth == 2
    assert meta_tokens.shape == (N_META, D_MODEL) and N_META >= MAX_WINDOW - 1
    params = pltpu.CompilerParams(dimension_semantics=("arbitrary", "arbitrary"),
                                  vmem_limit_bytes=VMEM_LIMIT_BYTES)

    meta_pad = jnp.concatenate([jnp.zeros((POOL_PAD, D_MODEL), x.dtype), meta_tokens.astype(x.dtype)], axis=0)
    pool_tiles = seq // POOL_TILE + 1

    def tile_spec(rows, first):
        return pl.BlockSpec((1, rows, D_MODEL), lambda b, i: (b, jnp.maximum(i + first, 0), 0))

    h1 = pl.pallas_call(
        _pool_layer_kernel,
        grid=(batch, pool_tiles),
        in_specs=[
            _const_spec((POOL_TILE, D_MODEL)),
            tile_spec(POOL_TILE, -1),
            _const_spec((1, D_MODEL)),
            _const_spec((D_MODEL, 2 * E_POOL)),
            _const_spec((len(POOL_WINDOWS), POOL_GROUP_DIM, POOL_GROUP_DIM)),
            _const_spec((1, E_POOL)),
            _const_spec((E_POOL, D_MODEL)),
        ],
        out_specs=tile_spec(POOL_TILE, 0),
        out_shape=jax.ShapeDtypeStruct((batch, pool_tiles * POOL_TILE, D_MODEL), _F32),
        scratch_shapes=[
            pltpu.VMEM((POOL_TILE + POOL_CARRY, E_POOL), _F32),
            pltpu.VMEM((POOL_CARRY, E_POOL), _F32),
        ],
        compiler_params=params,
        name="pool_layer",
    )(meta_pad, x, norm_w[0:1], pool_w_in[0].astype(_BF16), pool_w_grp[0].astype(_BF16), pool_scale[0:1],
      pool_w_out[0].astype(_BF16))

    hgrn_tiles = seq // ROW_TILE + 1
    n_chunks = ROW_TILE // CHUNK
    tile_f32 = pltpu.VMEM((ROW_TILE, E_HGRN), _F32)
    tile_bf16 = pltpu.VMEM((ROW_TILE, E_HGRN), _BF16)
    state = pltpu.VMEM((N_HEADS, HEAD_DIM, HEAD_DIM), _F32)
    hgrn_operands = (h1, norm_w[1:2], hgrn_w_in[0].astype(_BF16), hgrn_lb_logits, hgrn_o_norm[0:1],
                     hgrn_w_out[0].astype(_BF16), final_norm_w.reshape(1, D_MODEL))
    hgrn_in_specs = [
        tile_spec(ROW_TILE, POOL_TILE // ROW_TILE - 1),
        _const_spec((1, D_MODEL)),
        _const_spec((D_MODEL, 4 * E_HGRN)),
        _const_spec((depth, E_HGRN)),
        _const_spec((1, E_HGRN)),
        _const_spec((E_HGRN, D_MODEL)),
        _const_spec((1, D_MODEL)),
    ]
    hgrn_scratch = [
        pltpu.VMEM((ROW_TILE, D_MODEL), _BF16),
        pltpu.VMEM((1, E_HGRN), _F32),
        tile_bf16,
        tile_f32,
        tile_bf16,
        tile_f32,
        tile_bf16,
        tile_bf16,
        tile_bf16,
        tile_bf16,
        tile_f32,
        state,
    ]
    out_struct = jax.ShapeDtypeStruct((batch, seq, D_MODEL), _F32)

    out_fast, growth = pl.pallas_call(
        functools.partial(_hgrn_fast_kernel, 1),
        grid=(batch, hgrn_tiles),
        in_specs=hgrn_in_specs,
        out_specs=[tile_spec(ROW_TILE, -1),
                   pl.BlockSpec((1, 1, n_chunks, E_HGRN), lambda b, i: (b, i, 0, 0))],
        out_shape=[out_struct, jax.ShapeDtypeStruct((batch, hgrn_tiles, n_chunks, E_HGRN), _F32)],
        scratch_shapes=hgrn_scratch + [state],
        compiler_params=params,
        name="hgrn_layer",
    )(*hgrn_operands)

    def hgrn_safe():
        return pl.pallas_call(
            functools.partial(_hgrn_safe_kernel, 1),
            grid=(batch, hgrn_tiles),
            in_specs=hgrn_in_specs,
            out_specs=tile_spec(ROW_TILE, -1),
            out_shape=out_struct,
            scratch_shapes=hgrn_scratch + [
                state,
                pltpu.VMEM((n_chunks, E_HGRN), _F32),
                state,
            ],
            compiler_params=params,
            name="hgrn_layer_any_range",
        )(*hgrn_operands)

    in_range = jnp.max(growth) <= MAX_LOG2_GROWTH
    return lax.cond(in_range, lambda: out_fast, hgrn_safe)
```

```python
import functools

import jax
import jax.numpy as jnp
from jax import lax
from jax.experimental import pallas as pl
from jax.experimental.pallas import tpu as pltpu

D_MODEL = 1024
N_META = 16
E_POOL = 2048
POOL_WINDOWS = (2, 4, 8, 16)
POOL_GROUP_DIM = E_POOL // len(POOL_WINDOWS)
MAX_WINDOW = max(POOL_WINDOWS)
SUBLANES = 8
POOL_CARRY = SUBLANES * (MAX_WINDOW.bit_length() - 1)
E_HGRN = 2048
HEAD_DIM = 128
N_HEADS = E_HGRN // HEAD_DIM
EPS = 1e-6
LOG2_E = 1.4426950408889634

POOL_TILE = 512
POOL_PAD = POOL_TILE - N_META
HGRN_TILE = 512
HGRN_ANY_RANGE_TILE = 256
CHUNK = 128
HALF = CHUNK // 2
MAX_LOG2_GROWTH = 100.0
SMALL_STEP = 16
VMEM_LIMIT_BYTES = 56 * 1024 * 1024

_F32 = jnp.float32
_BF16 = jnp.bfloat16


def _rmsnorm(x, w):
    ms = jnp.mean(x * x, axis=-1, keepdims=True)
    return x * lax.rsqrt(ms + EPS) * w


def _silu(x):
    half = 0.5 * x
    return half + half * jnp.tanh(half)


def _dot(a, b):
    return jnp.dot(a, b, preferred_element_type=_F32)


def _dot_nt(a, b):
    return lax.dot_general(a, b, (((1,), (1,)), ((), ())), preferred_element_type=_F32)


def _dot_tn(a, b):
    return lax.dot_general(a, b, (((0,), (0,)), ((), ())), preferred_element_type=_F32)


def _pool_layer_kernel(meta_ref, x_ref, nw_ref, win_ref, wgrp_ref, scale_ref, wout_ref, out_ref, vbuf, meta_carry):
    row = pl.program_id(0)
    i = pl.program_id(1)
    weights = (nw_ref, win_ref, wgrp_ref, scale_ref, wout_ref)

    @pl.when((i == 0) & (row == 0))
    def _():
        vbuf[0:POOL_CARRY, :] = jnp.zeros((POOL_CARRY, E_POOL), _F32)
        _pool_tile(lambda: meta_ref[...], True, *weights, out_ref, vbuf)
        meta_carry[...] = vbuf[0:POOL_CARRY, :]

    @pl.when((i == 0) & (row > 0))
    def _():
        vbuf[0:POOL_CARRY, :] = meta_carry[...]
        out_ref[0] = jnp.zeros((POOL_TILE, D_MODEL), _F32)

    @pl.when(i > 0)
    def _():
        _pool_tile(lambda: x_ref[0], False, *weights, out_ref, vbuf)


def _pool_tile(read_h, is_meta_tile, nw_ref, win_ref, wgrp_ref, scale_ref, wout_ref, out_ref, vbuf):
    hn = _rmsnorm(read_h(), nw_ref[...]).astype(_BF16)
    v = _dot(hn, win_ref[:, 0:E_POOL])
    gate = _dot(hn, win_ref[:, E_POOL:2 * E_POOL])
    vbuf[POOL_CARRY:POOL_CARRY + POOL_TILE, :] = v

    parts = []
    for g, w in enumerate(POOL_WINDOWS):
        c0 = g * POOL_GROUP_DIM
        cols = slice(c0, c0 + POOL_GROUP_DIM)

        n_stages = w.bit_length() - 1
        acc = vbuf[POOL_CARRY - SUBLANES * n_stages:POOL_CARRY + POOL_TILE, cols]
        for stage in range(n_stages):
            m = 1 << stage
            acc = acc[SUBLANES:] + acc[SUBLANES - m:acc.shape[0] - m]
        if is_meta_tile:
            pos = lax.broadcasted_iota(jnp.int32, (POOL_TILE, 1), 0) - POOL_PAD
            inv_cnt = 1.0 / jnp.clip(pos + 1, 1, w).astype(_F32)
        else:
            inv_cnt = 1.0 / w
        u = acc * inv_cnt - vbuf[POOL_CARRY:POOL_CARRY + POOL_TILE, cols]
        parts.append(_dot(u.astype(_BF16), wgrp_ref[g]))
    u = jnp.concatenate(parts, axis=-1) * scale_ref[...]
    y = _dot((u * _silu(gate)).astype(_BF16), wout_ref[...])
    out_ref[0] = read_h() + y

    vbuf[0:POOL_CARRY, :] = vbuf[POOL_TILE:POOL_TILE + POOL_CARRY, :]


def _hgrn_prologue(layer, lbl_ref, lb_s):
    logits = lbl_ref[...]
    e = jnp.exp(logits - jnp.max(logits, axis=0, keepdims=True))
    p = e / jnp.sum(e, axis=0, keepdims=True)
    lb_s[...] = jnp.sum(p[0:layer + 1], axis=0, keepdims=True) - p[0:1]


def _hgrn_project(h_ref, nw_ref, win_ref, hn_s, q_s, g_s):
    hn = _rmsnorm(h_ref[0], nw_ref[...]).astype(_BF16)
    hn_s[...] = hn
    q_s[...] = _silu(_dot(hn, win_ref[:, 0:E_HGRN]).astype(_BF16))
    g_s[...] = _dot(hn, win_ref[:, E_HGRN:2 * E_HGRN])


def _hgrn_fast_kernel(layer, h_ref, nw_ref, win_ref, lbl_ref, onw_ref, wout_ref, fw_ref, out_ref, growth_ref,
                      hn_s, lb_s, q_s, k_s, v_s, g_s, gate_s, qin_s, kin_s, a_s, o_s, st_s, meta_state):
    row = pl.program_id(0)
    i = pl.program_id(1)
    reuse_meta = (i == 0) & (row > 0)

    @pl.when((i == 0) & (row == 0))
    def _():
        _hgrn_prologue(layer, lbl_ref, lb_s)

    @pl.when(reuse_meta)
    def _():
        st_s[...] = meta_state[...]
        growth_ref[...] = jnp.zeros(growth_ref.shape, _F32)

    @pl.when(jnp.logical_not(reuse_meta))
    def _():
        @pl.when(i == 0)
        def _():
            st_s[...] = jnp.zeros(st_s.shape, _F32)

        def store_growth(c, value):
            growth_ref[0, 0, c:c + 1, :] = value

        _hgrn_project(h_ref, nw_ref, win_ref, hn_s, q_s, g_s)
        _hgrn_mix(win_ref, hn_s, lb_s, q_s, k_s, v_s, g_s, gate_s, qin_s, kin_s, a_s, o_s, st_s, store_growth)
        _hgrn_output(h_ref, onw_ref, wout_ref, fw_ref, out_ref, gate_s, o_s)

        @pl.when(i == 0)
        def _():
            meta_state[...] = st_s[...]


def _hgrn_safe_kernel(layer, h_ref, nw_ref, win_ref, lbl_ref, onw_ref, wout_ref, fw_ref, out_ref,
                      hn_s, lb_s, q_s, k_s, v_s, g_s, gate_s, qin_s, kin_s, a_s, o_s, st_s, st0_s, growth_s,
                      meta_state):
    row = pl.program_id(0)
    i = pl.program_id(1)
    reuse_meta = (i == 0) & (row > 0)

    @pl.when((i == 0) & (row == 0))
    def _():
        _hgrn_prologue(layer, lbl_ref, lb_s)

    @pl.when(reuse_meta)
    def _():
        st_s[...] = meta_state[...]
        st0_s[...] = meta_state[...]

    @pl.when(jnp.logical_not(reuse_meta))
    def _():
        @pl.when(i == 0)
        def _():
            st_s[...] = jnp.zeros(st_s.shape, _F32)
            st0_s[...] = jnp.zeros(st_s.shape, _F32)

        _hgrn_project(h_ref, nw_ref, win_ref, hn_s, q_s, g_s)

    @pl.when((i > 0) | (row == 0))
    def _():
        def store_growth(c, value):
            growth_s[c:c + 1, :] = value

        _hgrn_mix(win_ref, hn_s, lb_s, q_s, k_s, v_s, g_s, gate_s, qin_s, kin_s, a_s, o_s, st_s, store_growth)
        _hgrn_output(h_ref, onw_ref, wout_ref, fw_ref, out_ref, gate_s, o_s)

        in_range = jnp.max(growth_s[...]) <= MAX_LOG2_GROWTH

        @pl.when(jnp.logical_not(in_range))
        def _():
            st_s[...] = st0_s[...]
            _hgrn_recurrence_small_steps(q_s, k_s, v_s, g_s, o_s, st_s)
            _hgrn_output(h_ref, onw_ref, wout_ref, fw_ref, out_ref, gate_s, o_s)

        st0_s[...] = st_s[...]

        @pl.when(i == 0)
        def _():
            meta_state[...] = st_s[...]


def _hgrn_mix(win_ref, hn_s, lb_s, q_s, k_s, v_s, g_s, gate_s, qin_s, kin_s, a_s, o_s, st_s, store_growth):
    lb = lb_s[...]
    hn = hn_s[...]
    v_s[...] = _dot(hn, win_ref[:, 2 * E_HGRN:3 * E_HGRN]).astype(_BF16)
    gate_s[...] = _silu(_dot(hn, win_ref[:, 3 * E_HGRN:4 * E_HGRN]).astype(_BF16))

    fp = g_s[...]
    t = jnp.exp2(jnp.abs(fp) * (-LOG2_E))
    r = 1.0 / (1.0 + t)
    tr = t * r
    sig_pos = jnp.where(fp >= 0, r, tr)
    sig_neg = jnp.where(fp >= 0, tr, r)
    g_s[...] = jnp.log2(lb + (1.0 - lb) * sig_pos)
    k_s[...] = (1.0 - lb) * sig_neg

    row = lax.broadcasted_iota(jnp.int32, (CHUNK, CHUNK), 0)
    col = lax.broadcasted_iota(jnp.int32, (CHUNK, CHUNK), 1)
    causal = col <= row
    tri = causal.astype(_BF16)
    tri2 = jnp.concatenate([tri, tri], axis=1)

    d_start_mid, d_mid_end, d_start_end = [], [], []
    n_chunks = q_s.shape[0] // CHUNK
    for c in range(n_chunks):
        rs = slice(c * CHUNK, (c + 1) * CHUNK)
        g = g_s[rs, :]
        g_hi = g.astype(_BF16)
        g_lo = (g - g_hi.astype(_F32)).astype(_BF16)
        b = _dot(tri2, jnp.concatenate([g_hi, g_lo], axis=0))
        b_mid = b[HALF - 1:HALF, :]
        b_end = b[CHUNK - 1:CHUNK, :]
        qin_s[rs, :] = q_s[rs, :] * jnp.exp2(b - b_mid).astype(_BF16)
        kin_s[rs, :] = (k_s[rs, :] * jnp.exp2(b_mid - b)).astype(_BF16)
        d_start_mid.append(jnp.exp2(b_mid))
        d_mid_end.append(jnp.exp2(b_end - b_mid))
        d_start_end.append(jnp.exp2(b_end))
        store_growth(c, jnp.maximum(b[0:1, :] - b_mid, b_mid - b_end))

    for c in range(n_chunks):
        rs = slice(c * CHUNK, (c + 1) * CHUNK)
        for hd in range(N_HEADS):
            ls = slice(hd * HEAD_DIM, (hd + 1) * HEAD_DIM)
            a = _dot_nt(qin_s[rs, ls], kin_s[rs, ls])
            a_s[rs, ls] = jnp.where(causal, a, 0.0).astype(_BF16)

    for c in range(n_chunks):
        rs = slice(c * CHUNK, (c + 1) * CHUNK)
        for hd in range(N_HEADS):
            ls = slice(hd * HEAD_DIM, (hd + 1) * HEAD_DIM)
            st = st_s[hd]
            v = v_s[rs, ls]
            k_mid = kin_s[rs, ls]
            s_mid = jnp.transpose(st * d_start_mid[c][:, ls]).astype(_BF16)
            lhs = jnp.concatenate([qin_s[rs, ls], a_s[rs, ls]], axis=1)
            rhs = jnp.concatenate([s_mid, v], axis=0)
            o_s[rs, ls] = _dot(lhs, rhs)
            st_s[hd] = st * d_start_end[c][:, ls] + _dot_tn(v, k_mid) * d_mid_end[c][:, ls]


def _hgrn_output(h_ref, onw_ref, wout_ref, fw_ref, out_ref, gate_s, o_s):
    parts = []
    for hd in range(N_HEADS):
        ls = slice(hd * HEAD_DIM, (hd + 1) * HEAD_DIM)
        o = o_s[:, ls]
        parts.append(o * lax.rsqrt(jnp.mean(o * o, axis=-1, keepdims=True) + EPS))
    o = jnp.concatenate(parts, axis=-1) * onw_ref[...]
    y = _dot(o.astype(_BF16) * gate_s[...], wout_ref[...])
    out_ref[0] = _rmsnorm(h_ref[0] + y, fw_ref[...])


def _hgrn_recurrence_small_steps(q_s, k_s, v_s, g_s, o_s, st_s):
    step_row = lax.broadcasted_iota(jnp.int32, (SMALL_STEP, 1), 0)
    r_i = lax.broadcasted_iota(jnp.int32, (SMALL_STEP, SMALL_STEP), 0)
    c_i = lax.broadcasted_iota(jnp.int32, (SMALL_STEP, SMALL_STEP), 1)
    tri = (c_i <= r_i).astype(_BF16)
    tri3 = jnp.concatenate([tri, tri, tri], axis=1)

    def body(j, carry):
        rs = pl.ds(pl.multiple_of(j * SMALL_STEP, SMALL_STEP), SMALL_STEP)
        for hd in range(N_HEADS):
            ls = slice(hd * HEAD_DIM, (hd + 1) * HEAD_DIM)
            g = g_s[rs, ls]
            g_hi = g.astype(_BF16)
            g_md = (g - g_hi.astype(_F32)).astype(_BF16)
            g_lo = (g - g_hi.astype(_F32) - g_md.astype(_F32)).astype(_BF16)
            b = _dot(tri3, jnp.concatenate([g_hi, g_md, g_lo], axis=0))
            b_end = b[SMALL_STEP - 1:SMALL_STEP, :]
            q = q_s[rs, ls].astype(_F32)
            k = k_s[rs, ls]
            v = v_s[rs, ls].astype(_F32)
            st = st_s[hd]
            o = _dot_nt((q * jnp.exp2(b)).astype(_BF16), st.astype(_BF16))
            rows = []
            for t in range(SMALL_STEP):
                rel = jnp.where(step_row <= t, b[t:t + 1, :] - b, -jnp.inf)
                score = jnp.sum(q[t:t + 1, :] * k * jnp.exp2(rel), axis=-1, keepdims=True)
                rows.append(jnp.sum(score * v, axis=0, keepdims=True))
            o_s[rs, ls] = o + jnp.concatenate(rows, axis=0)
            k_end = (k * jnp.exp2(b_end - b)).astype(_BF16)
            st_s[hd] = st * jnp.exp2(b_end) + _dot_tn(v.astype(_BF16), k_end)
        return carry

    lax.fori_loop(0, q_s.shape[0] // SMALL_STEP, body, 0)


def _const_spec(shape):
    zeros = (0,) * len(shape)
    return pl.BlockSpec(shape, lambda b, i: zeros, pipeline_mode=pl.Buffered(1))


def kernel(x, meta_tokens, norm_w, pool_w_in, pool_w_grp, pool_scale, pool_w_out, hgrn_w_in, hgrn_lb_logits,
           hgrn_o_norm, hgrn_w_out, final_norm_w):
    batch, seq, d = x.shape
    depth = norm_w.shape[0]
    assert d == D_MODEL and seq % POOL_TILE == 0 and depth == 2
    assert POOL_TILE % HGRN_TILE == 0 and POOL_TILE % HGRN_ANY_RANGE_TILE == 0
    assert meta_tokens.shape == (N_META, D_MODEL) and N_META >= MAX_WINDOW - 1
    params = pltpu.CompilerParams(dimension_semantics=("arbitrary", "arbitrary"),
                                  vmem_limit_bytes=VMEM_LIMIT_BYTES)

    meta_pad = jnp.concatenate([jnp.zeros((POOL_PAD, D_MODEL), x.dtype), meta_tokens.astype(x.dtype)], axis=0)
    pool_tiles = seq // POOL_TILE + 1

    def tile_spec(rows, first):
        return pl.BlockSpec((1, rows, D_MODEL), lambda b, i: (b, jnp.maximum(i + first, 0), 0))

    h1 = pl.pallas_call(
        _pool_layer_kernel,
        grid=(batch, pool_tiles),
        in_specs=[
            _const_spec((POOL_TILE, D_MODEL)),
            tile_spec(POOL_TILE, -1),
            _const_spec((1, D_MODEL)),
            _const_spec((D_MODEL, 2 * E_POOL)),
            _const_spec((len(POOL_WINDOWS), POOL_GROUP_DIM, POOL_GROUP_DIM)),
            _const_spec((1, E_POOL)),
            _const_spec((E_POOL, D_MODEL)),
        ],
        out_specs=tile_spec(POOL_TILE, 0),
        out_shape=jax.ShapeDtypeStruct((batch, pool_tiles * POOL_TILE, D_MODEL), _F32),
        scratch_shapes=[
            pltpu.VMEM((POOL_TILE + POOL_CARRY, E_POOL), _F32),
            pltpu.VMEM((POOL_CARRY, E_POOL), _F32),
        ],
        compiler_params=params,
        name="pool_layer",
    )(meta_pad, x, norm_w[0:1], pool_w_in[0].astype(_BF16), pool_w_grp[0].astype(_BF16), pool_scale[0:1],
      pool_w_out[0].astype(_BF16))

    state = pltpu.VMEM((N_HEADS, HEAD_DIM, HEAD_DIM), _F32)
    hgrn_operands = (h1, norm_w[1:2], hgrn_w_in[0].astype(_BF16), hgrn_lb_logits, hgrn_o_norm[0:1],
                     hgrn_w_out[0].astype(_BF16), final_norm_w.reshape(1, D_MODEL))
    out_struct = jax.ShapeDtypeStruct((batch, seq, D_MODEL), _F32)

    def hgrn_call(kernel_fn, rows, name, reports_growth, extra_scratch):
        n_tiles = seq // rows + 1
        n_chunks = rows // CHUNK
        tile_f32 = pltpu.VMEM((rows, E_HGRN), _F32)
        tile_bf16 = pltpu.VMEM((rows, E_HGRN), _BF16)
        out_specs, out_shape = tile_spec(rows, -1), out_struct
        if reports_growth:
            out_specs = [out_specs, pl.BlockSpec((1, 1, n_chunks, E_HGRN), lambda b, i: (b, i, 0, 0))]
            out_shape = [out_shape, jax.ShapeDtypeStruct((batch, n_tiles, n_chunks, E_HGRN), _F32)]
        return pl.pallas_call(
            functools.partial(kernel_fn, 1),
            grid=(batch, n_tiles),
            in_specs=[
                tile_spec(rows, POOL_TILE // rows - 1),
                _const_spec((1, D_MODEL)),
                _const_spec((D_MODEL, 4 * E_HGRN)),
                _const_spec((depth, E_HGRN)),
                _const_spec((1, E_HGRN)),
                _const_spec((E_HGRN, D_MODEL)),
                _const_spec((1, D_MODEL)),
            ],
            out_specs=out_specs,
            out_shape=out_shape,
            scratch_shapes=[
                pltpu.VMEM((rows, D_MODEL), _BF16),
                pltpu.VMEM((1, E_HGRN), _F32),
                tile_bf16,
                tile_f32,
                tile_bf16,
                tile_f32,
                tile_bf16,
                tile_bf16,
                tile_bf16,
                tile_bf16,
                tile_f32,
                state,
            ] + extra_scratch(n_chunks),
            compiler_params=params,
            name=name,
        )(*hgrn_operands)

    out_fast, growth = hgrn_call(_hgrn_fast_kernel, HGRN_TILE, "hgrn_layer", True,
                                 lambda n_chunks: [state])

    def hgrn_any_range():
        return hgrn_call(_hgrn_safe_kernel, HGRN_ANY_RANGE_TILE, "hgrn_layer_any_range", False, lambda n_chunks: [
            state,
            pltpu.VMEM((n_chunks, E_HGRN), _F32),
            state,
        ])

    in_range = jnp.max(growth) <= MAX_LOG2_GROWTH
    return lax.cond(in_range, lambda: out_fast, hgrn_any_range)
```

```python
import functools

import jax
import jax.numpy as jnp
from jax import lax
from jax.experimental import pallas as pl
from jax.experimental.pallas import tpu as pltpu

D_MODEL = 1024
N_META = 16
E_POOL = 2048
POOL_WINDOWS = (2, 4, 8, 16)
POOL_GROUP_DIM = E_POOL // len(POOL_WINDOWS)
MAX_WINDOW = max(POOL_WINDOWS)
SUBLANES = 8
POOL_CARRY = SUBLANES * (MAX_WINDOW.bit_length() - 1)
E_HGRN = 2048
HEAD_DIM = 128
N_HEADS = E_HGRN // HEAD_DIM
EPS = 1e-6
LOG2_E = 1.4426950408889634

POOL_TILE = 512
POOL_PAD = POOL_TILE - N_META
HGRN_TILE = 512
HGRN_ANY_RANGE_TILE = 256
CHUNK = 128
HALF = CHUNK // 2
COL_TILE = 256
MAX_LOG2_GROWTH = 100.0
SMALL_STEP = 16
VMEM_LIMIT_BYTES = 56 * 1024 * 1024

_F32 = jnp.float32
_BF16 = jnp.bfloat16


def _rmsnorm(x, w):
    ms = jnp.mean(x * x, axis=-1, keepdims=True)
    return x * lax.rsqrt(ms + EPS) * w


def _silu(x):
    half = 0.5 * x
    return half + half * jnp.tanh(half)


def _dot(a, b):
    return jnp.dot(a, b, preferred_element_type=_F32)


def _dot_nt(a, b):
    return lax.dot_general(a, b, (((1,), (1,)), ((), ())), preferred_element_type=_F32)


def _dot_tn(a, b):
    return lax.dot_general(a, b, (((0,), (0,)), ((), ())), preferred_element_type=_F32)


def _pool_layer_kernel(meta_ref, x_ref, nw_ref, win_ref, wgrp_ref, scale_ref, wout_ref, out_ref, vbuf, meta_carry):
    row = pl.program_id(0)
    i = pl.program_id(1)
    weights = (nw_ref, win_ref, wgrp_ref, scale_ref, wout_ref)

    @pl.when((i == 0) & (row == 0))
    def _():
        vbuf[0:POOL_CARRY, :] = jnp.zeros((POOL_CARRY, E_POOL), _F32)
        _pool_tile(lambda: meta_ref[...], True, *weights, out_ref, vbuf)
        meta_carry[...] = vbuf[0:POOL_CARRY, :]

    @pl.when((i == 0) & (row > 0))
    def _():
        vbuf[0:POOL_CARRY, :] = meta_carry[...]
        out_ref[0] = jnp.zeros((POOL_TILE, D_MODEL), _F32)

    @pl.when(i > 0)
    def _():
        _pool_tile(lambda: x_ref[0], False, *weights, out_ref, vbuf)


def _pool_tile(read_h, is_meta_tile, nw_ref, win_ref, wgrp_ref, scale_ref, wout_ref, out_ref, vbuf):
    hn = _rmsnorm(read_h(), nw_ref[...]).astype(_BF16)
    v = _dot(hn, win_ref[:, 0:E_POOL])
    gate = _dot(hn, win_ref[:, E_POOL:2 * E_POOL])
    vbuf[POOL_CARRY:POOL_CARRY + POOL_TILE, :] = v

    parts = []
    for g, w in enumerate(POOL_WINDOWS):
        c0 = g * POOL_GROUP_DIM
        cols = slice(c0, c0 + POOL_GROUP_DIM)

        n_stages = w.bit_length() - 1
        acc = vbuf[POOL_CARRY - SUBLANES * n_stages:POOL_CARRY + POOL_TILE, cols]
        for stage in range(n_stages):
            m = 1 << stage
            acc = acc[SUBLANES:] + acc[SUBLANES - m:acc.shape[0] - m]
        if is_meta_tile:
            pos = lax.broadcasted_iota(jnp.int32, (POOL_TILE, 1), 0) - POOL_PAD
            inv_cnt = 1.0 / jnp.clip(pos + 1, 1, w).astype(_F32)
        else:
            inv_cnt = 1.0 / w
        u = acc * inv_cnt - vbuf[POOL_CARRY:POOL_CARRY + POOL_TILE, cols]
        parts.append(_dot(u.astype(_BF16), wgrp_ref[g]))
    u = jnp.concatenate(parts, axis=-1) * scale_ref[...]
    y = _dot((u * _silu(gate)).astype(_BF16), wout_ref[...])
    out_ref[0] = read_h() + y

    vbuf[0:POOL_CARRY, :] = vbuf[POOL_TILE:POOL_TILE + POOL_CARRY, :]


def _hgrn_prologue(layer, lbl_ref, lb_s):
    logits = lbl_ref[...]
    e = jnp.exp(logits - jnp.max(logits, axis=0, keepdims=True))
    p = e / jnp.sum(e, axis=0, keepdims=True)
    lb_s[...] = jnp.sum(p[0:layer + 1], axis=0, keepdims=True) - p[0:1]


def _hgrn_project(h_ref, nw_ref, win_ref, hn_s, q_s, g_s):
    hn = _rmsnorm(h_ref[0], nw_ref[...]).astype(_BF16)
    hn_s[...] = hn
    _store_cols(q_s, slice(None), _silu(_dot(hn, win_ref[:, 0:E_HGRN]).astype(_BF16)))
    _store_cols(g_s, slice(None), _dot(hn, win_ref[:, E_HGRN:2 * E_HGRN]))


def _hgrn_fast_kernel(layer, h_ref, nw_ref, win_ref, lbl_ref, onw_ref, wout_ref, fw_ref, out_ref, growth_ref,
                      hn_s, lb_s, q_s, k_s, v_s, g_s, gate_s, qin_s, kin_s, a_s, o_s, st_s, meta_state):
    row = pl.program_id(0)
    i = pl.program_id(1)
    reuse_meta = (i == 0) & (row > 0)

    @pl.when((i == 0) & (row == 0))
    def _():
        _hgrn_prologue(layer, lbl_ref, lb_s)

    @pl.when(reuse_meta)
    def _():
        st_s[...] = meta_state[...]
        growth_ref[...] = jnp.zeros(growth_ref.shape, _F32)

    @pl.when(jnp.logical_not(reuse_meta))
    def _():
        @pl.when(i == 0)
        def _():
            st_s[...] = jnp.zeros(st_s.shape, _F32)

        def store_growth(c, value):
            growth_ref[0, 0, c:c + 1, :] = value

        _hgrn_project(h_ref, nw_ref, win_ref, hn_s, q_s, g_s)
        _hgrn_mix(win_ref, hn_s, lb_s, q_s, k_s, v_s, g_s, gate_s, qin_s, kin_s, a_s, o_s, st_s, store_growth)
        _hgrn_output(h_ref, onw_ref, wout_ref, fw_ref, out_ref, gate_s, o_s)

        @pl.when(i == 0)
        def _():
            meta_state[...] = st_s[...]


def _hgrn_safe_kernel(layer, h_ref, nw_ref, win_ref, lbl_ref, onw_ref, wout_ref, fw_ref, out_ref,
                      hn_s, lb_s, q_s, k_s, v_s, g_s, gate_s, qin_s, kin_s, a_s, o_s, st_s, st0_s, growth_s,
                      meta_state):
    row = pl.program_id(0)
    i = pl.program_id(1)
    reuse_meta = (i == 0) & (row > 0)

    @pl.when((i == 0) & (row == 0))
    def _():
        _hgrn_prologue(layer, lbl_ref, lb_s)

    @pl.when(reuse_meta)
    def _():
        st_s[...] = meta_state[...]
        st0_s[...] = meta_state[...]

    @pl.when(jnp.logical_not(reuse_meta))
    def _():
        @pl.when(i == 0)
        def _():
            st_s[...] = jnp.zeros(st_s.shape, _F32)
            st0_s[...] = jnp.zeros(st_s.shape, _F32)

        _hgrn_project(h_ref, nw_ref, win_ref, hn_s, q_s, g_s)

    @pl.when((i > 0) | (row == 0))
    def _():
        def store_growth(c, value):
            growth_s[c:c + 1, :] = value

        _hgrn_mix(win_ref, hn_s, lb_s, q_s, k_s, v_s, g_s, gate_s, qin_s, kin_s, a_s, o_s, st_s, store_growth)
        _hgrn_output(h_ref, onw_ref, wout_ref, fw_ref, out_ref, gate_s, o_s)

        in_range = jnp.max(growth_s[...]) <= MAX_LOG2_GROWTH

        @pl.when(jnp.logical_not(in_range))
        def _():
            st_s[...] = st0_s[...]
            _hgrn_recurrence_small_steps(q_s, k_s, v_s, g_s, o_s, st_s)
            _hgrn_output(h_ref, onw_ref, wout_ref, fw_ref, out_ref, gate_s, o_s)

        st0_s[...] = st_s[...]

        @pl.when(i == 0)
        def _():
            meta_state[...] = st_s[...]


def _store_cols(ref, rows, value):
    for j in range(ref.shape[0]):
        ref[j, rows] = value[:, j * COL_TILE:(j + 1) * COL_TILE]


def _load_cols(ref, rows):
    return jnp.concatenate([ref[j, rows] for j in range(ref.shape[0])], axis=1)


def _load_head(ref, rows, hd):
    heads_per_tile = COL_TILE // HEAD_DIM
    lane0 = (hd % heads_per_tile) * HEAD_DIM
    return ref[hd // heads_per_tile, rows, lane0:lane0 + HEAD_DIM]


def _store_heads(ref, rows, value):
    for hd in range(N_HEADS):
        ref[hd, rows] = value[:, hd * HEAD_DIM:(hd + 1) * HEAD_DIM]


def _hgrn_mix(win_ref, hn_s, lb_s, q_s, k_s, v_s, g_s, gate_s, qin_s, kin_s, a_s, o_s, st_s, store_growth):
    lb = lb_s[...]
    hn = hn_s[...]
    _store_heads(v_s, slice(None), _dot(hn, win_ref[:, 2 * E_HGRN:3 * E_HGRN]).astype(_BF16))
    _store_cols(gate_s, slice(None), _silu(_dot(hn, win_ref[:, 3 * E_HGRN:4 * E_HGRN]).astype(_BF16)))

    fp = _load_cols(g_s, slice(None))
    t = jnp.exp2(jnp.abs(fp) * (-LOG2_E))
    r = 1.0 / (1.0 + t)
    tr = t * r
    sig_pos = jnp.where(fp >= 0, r, tr)
    sig_neg = jnp.where(fp >= 0, tr, r)
    _store_cols(g_s, slice(None), jnp.log2(lb + (1.0 - lb) * sig_pos))
    _store_cols(k_s, slice(None), (1.0 - lb) * sig_neg)

    row = lax.broadcasted_iota(jnp.int32, (CHUNK, CHUNK), 0)
    col = lax.broadcasted_iota(jnp.int32, (CHUNK, CHUNK), 1)
    causal = col <= row
    tri = causal.astype(_BF16)
    tri2 = jnp.concatenate([tri, tri], axis=1)

    d_start_mid, d_mid_end, d_start_end = [], [], []
    n_chunks = q_s.shape[1] // CHUNK
    for c in range(n_chunks):
        rs = slice(c * CHUNK, (c + 1) * CHUNK)
        g = _load_cols(g_s, rs)
        g_hi = g.astype(_BF16)
        g_lo = (g - g_hi.astype(_F32)).astype(_BF16)
        b = _dot(tri2, jnp.concatenate([g_hi, g_lo], axis=0))
        b_mid = b[HALF - 1:HALF, :]
        b_end = b[CHUNK - 1:CHUNK, :]
        _store_heads(qin_s, rs, _load_cols(q_s, rs) * jnp.exp2(b - b_mid).astype(_BF16))
        _store_heads(kin_s, rs, (_load_cols(k_s, rs) * jnp.exp2(b_mid - b)).astype(_BF16))
        d_start_mid.append(jnp.exp2(b_mid))
        d_mid_end.append(jnp.exp2(b_end - b_mid))
        d_start_end.append(jnp.exp2(b_end))
        store_growth(c, jnp.maximum(b[0:1, :] - b_mid, b_mid - b_end))

    for c in range(n_chunks):
        rs = slice(c * CHUNK, (c + 1) * CHUNK)
        for hd in range(N_HEADS):
            ls = slice(hd * HEAD_DIM, (hd + 1) * HEAD_DIM)
            a = _dot_nt(qin_s[hd, rs], kin_s[hd, rs])
            a_s[hd, rs] = jnp.where(causal, a, 0.0).astype(_BF16)

    for c in range(n_chunks):
        rs = slice(c * CHUNK, (c + 1) * CHUNK)
        for hd in range(N_HEADS):
            ls = slice(hd * HEAD_DIM, (hd + 1) * HEAD_DIM)
            st = st_s[hd]
            v = v_s[hd, rs]
            k_mid = kin_s[hd, rs]
            s_mid = jnp.transpose(st * d_start_mid[c][:, ls]).astype(_BF16)
            lhs = jnp.concatenate([qin_s[hd, rs], a_s[hd, rs]], axis=1)
            rhs = jnp.concatenate([s_mid, v], axis=0)
            o_s[hd, rs] = _dot(lhs, rhs)
            st_s[hd] = st * d_start_end[c][:, ls] + _dot_tn(v, k_mid) * d_mid_end[c][:, ls]


def _hgrn_output(h_ref, onw_ref, wout_ref, fw_ref, out_ref, gate_s, o_s):
    parts = []
    for hd in range(N_HEADS):
        ls = slice(hd * HEAD_DIM, (hd + 1) * HEAD_DIM)
        o = o_s[hd]
        parts.append(o * lax.rsqrt(jnp.mean(o * o, axis=-1, keepdims=True) + EPS))
    o = jnp.concatenate(parts, axis=-1) * onw_ref[...]
    y = _dot(o.astype(_BF16) * _load_cols(gate_s, slice(None)), wout_ref[...])
    out_ref[0] = _rmsnorm(h_ref[0] + y, fw_ref[...])


def _hgrn_recurrence_small_steps(q_s, k_s, v_s, g_s, o_s, st_s):
    step_row = lax.broadcasted_iota(jnp.int32, (SMALL_STEP, 1), 0)
    r_i = lax.broadcasted_iota(jnp.int32, (SMALL_STEP, SMALL_STEP), 0)
    c_i = lax.broadcasted_iota(jnp.int32, (SMALL_STEP, SMALL_STEP), 1)
    tri = (c_i <= r_i).astype(_BF16)
    tri3 = jnp.concatenate([tri, tri, tri], axis=1)

    def body(j, carry):
        rs = pl.ds(pl.multiple_of(j * SMALL_STEP, SMALL_STEP), SMALL_STEP)
        for hd in range(N_HEADS):
            ls = slice(hd * HEAD_DIM, (hd + 1) * HEAD_DIM)
            g = _load_head(g_s, rs, hd)
            g_hi = g.astype(_BF16)
            g_md = (g - g_hi.astype(_F32)).astype(_BF16)
            g_lo = (g - g_hi.astype(_F32) - g_md.astype(_F32)).astype(_BF16)
            b = _dot(tri3, jnp.concatenate([g_hi, g_md, g_lo], axis=0))
            b_end = b[SMALL_STEP - 1:SMALL_STEP, :]
            q = _load_head(q_s, rs, hd).astype(_F32)
            k = _load_head(k_s, rs, hd)
            v = v_s[hd, rs].astype(_F32)
            st = st_s[hd]
            o = _dot_nt((q * jnp.exp2(b)).astype(_BF16), st.astype(_BF16))
            rows = []
            for t in range(SMALL_STEP):
                rel = jnp.where(step_row <= t, b[t:t + 1, :] - b, -jnp.inf)
                score = jnp.sum(q[t:t + 1, :] * k * jnp.exp2(rel), axis=-1, keepdims=True)
                rows.append(jnp.sum(score * v, axis=0, keepdims=True))
            o_s[hd, rs] = o + jnp.concatenate(rows, axis=0)
            k_end = (k * jnp.exp2(b_end - b)).astype(_BF16)
            st_s[hd] = st * jnp.exp2(b_end) + _dot_tn(v.astype(_BF16), k_end)
        return carry

    lax.fori_loop(0, q_s.shape[1] // SMALL_STEP, body, 0)


def _const_spec(shape):
    zeros = (0,) * len(shape)
    return pl.BlockSpec(shape, lambda b, i: zeros, pipeline_mode=pl.Buffered(1))


def kernel(x, meta_tokens, norm_w, pool_w_in, pool_w_grp, pool_scale, pool_w_out, hgrn_w_in, hgrn_lb_logits,
           hgrn_o_norm, hgrn_w_out, final_norm_w):
    batch, seq, d = x.shape
    depth = norm_w.shape[0]
    assert d == D_MODEL and seq % POOL_TILE == 0 and depth == 2
    assert POOL_TILE % HGRN_TILE == 0 and POOL_TILE % HGRN_ANY_RANGE_TILE == 0
    assert meta_tokens.shape == (N_META, D_MODEL) and N_META >= MAX_WINDOW - 1
    params = pltpu.CompilerParams(dimension_semantics=("arbitrary", "arbitrary"),
                                  vmem_limit_bytes=VMEM_LIMIT_BYTES)

    meta_pad = jnp.concatenate([jnp.zeros((POOL_PAD, D_MODEL), x.dtype), meta_tokens.astype(x.dtype)], axis=0)
    pool_tiles = seq // POOL_TILE + 1

    def tile_spec(rows, first):
        return pl.BlockSpec((1, rows, D_MODEL), lambda b, i: (b, jnp.maximum(i + first, 0), 0))

    h1 = pl.pallas_call(
        _pool_layer_kernel,
        grid=(batch, pool_tiles),
        in_specs=[
            _const_spec((POOL_TILE, D_MODEL)),
            tile_spec(POOL_TILE, -1),
            _const_spec((1, D_MODEL)),
            _const_spec((D_MODEL, 2 * E_POOL)),
            _const_spec((len(POOL_WINDOWS), POOL_GROUP_DIM, POOL_GROUP_DIM)),
            _const_spec((1, E_POOL)),
            _const_spec((E_POOL, D_MODEL)),
        ],
        out_specs=tile_spec(POOL_TILE, 0),
        out_shape=jax.ShapeDtypeStruct((batch, pool_tiles * POOL_TILE, D_MODEL), _F32),
        scratch_shapes=[
            pltpu.VMEM((POOL_TILE + POOL_CARRY, E_POOL), _F32),
            pltpu.VMEM((POOL_CARRY, E_POOL), _F32),
        ],
        compiler_params=params,
        name="pool_layer",
    )(meta_pad, x, norm_w[0:1], pool_w_in[0].astype(_BF16), pool_w_grp[0].astype(_BF16), pool_scale[0:1],
      pool_w_out[0].astype(_BF16))

    state = pltpu.VMEM((N_HEADS, HEAD_DIM, HEAD_DIM), _F32)
    hgrn_operands = (h1, norm_w[1:2], hgrn_w_in[0].astype(_BF16), hgrn_lb_logits, hgrn_o_norm[0:1],
                     hgrn_w_out[0].astype(_BF16), final_norm_w.reshape(1, D_MODEL))
    out_struct = jax.ShapeDtypeStruct((batch, seq, D_MODEL), _F32)

    def hgrn_call(kernel_fn, rows, name, reports_growth, extra_scratch):
        n_tiles = seq // rows + 1
        n_chunks = rows // CHUNK
        cols_f32 = pltpu.VMEM((E_HGRN // COL_TILE, rows, COL_TILE), _F32)
        cols_bf16 = pltpu.VMEM((E_HGRN // COL_TILE, rows, COL_TILE), _BF16)
        heads_bf16 = pltpu.VMEM((N_HEADS, rows, HEAD_DIM), _BF16)
        out_specs, out_shape = tile_spec(rows, -1), out_struct
        if reports_growth:
            out_specs = [out_specs, pl.BlockSpec((1, 1, n_chunks, E_HGRN), lambda b, i: (b, i, 0, 0))]
            out_shape = [out_shape, jax.ShapeDtypeStruct((batch, n_tiles, n_chunks, E_HGRN), _F32)]
        return pl.pallas_call(
            functools.partial(kernel_fn, 1),
            grid=(batch, n_tiles),
            in_specs=[
                tile_spec(rows, POOL_TILE // rows - 1),
                _const_spec((1, D_MODEL)),
                _const_spec((D_MODEL, 4 * E_HGRN)),
                _const_spec((depth, E_HGRN)),
                _const_spec((1, E_HGRN)),
                _const_spec((E_HGRN, D_MODEL)),
                _const_spec((1, D_MODEL)),
            ],
            out_specs=out_specs,
            out_shape=out_shape,
            scratch_shapes=[
                pltpu.VMEM((rows, D_MODEL), _BF16),
                pltpu.VMEM((1, E_HGRN), _F32),
                cols_bf16,
                cols_f32,
                heads_bf16,
                cols_f32,
                cols_bf16,
                heads_bf16,
                heads_bf16,
                heads_bf16,
                pltpu.VMEM((N_HEADS, rows, HEAD_DIM), _F32),
                state,
            ] + extra_scratch(n_chunks),
            compiler_params=params,
            name=name,
        )(*hgrn_operands)

    out_fast, growth = hgrn_call(_hgrn_fast_kernel, HGRN_TILE, "hgrn_layer", True,
                                 lambda n_chunks: [state])

    def hgrn_any_range():
        return hgrn_call(_hgrn_safe_kernel, HGRN_ANY_RANGE_TILE, "hgrn_layer_any_range", False, lambda n_chunks: [
            state,
            pltpu.VMEM((n_chunks, E_HGRN), _F32),
            state,
        ])

    in_range = jnp.max(growth) <= MAX_LOG2_GROWTH
    return lax.cond(in_range, lambda: out_fast, hgrn_any_range)
```

```python
import functools

import jax
import jax.numpy as jnp
from jax import lax
from jax.experimental import pallas as pl
from jax.experimental.pallas import tpu as pltpu

D_MODEL = 1024
N_META = 16
E_POOL = 2048
POOL_WINDOWS = (2, 4, 8, 16)
POOL_GROUP_DIM = E_POOL // len(POOL_WINDOWS)
MAX_WINDOW = max(POOL_WINDOWS)
SUBLANES = 8
POOL_CARRY = SUBLANES * (MAX_WINDOW.bit_length() - 1)
E_HGRN = 2048
HEAD_DIM = 128
N_HEADS = E_HGRN // HEAD_DIM
EPS = 1e-6
LOG2_E = 1.4426950408889634

POOL_TILE = 512
POOL_PAD = POOL_TILE - N_META
HGRN_TILE = 512
HGRN_ANY_RANGE_TILE = 256
CHUNK = 128
HALF = CHUNK // 2
MAX_LOG2_GROWTH = 100.0
SMALL_STEP = 16
VMEM_LIMIT_BYTES = 56 * 1024 * 1024

_F32 = jnp.float32
_BF16 = jnp.bfloat16


def _rmsnorm(x, w):
    ms = jnp.mean(x * x, axis=-1, keepdims=True)
    return x * lax.rsqrt(ms + EPS) * w


def _silu(x):
    half = 0.5 * x
    return half + half * jnp.tanh(half)


def _dot(a, b):
    return jnp.dot(a, b, preferred_element_type=_F32)


def _dot_nt(a, b):
    return lax.dot_general(a, b, (((1,), (1,)), ((), ())), preferred_element_type=_F32)


def _dot_tn(a, b):
    return lax.dot_general(a, b, (((0,), (0,)), ((), ())), preferred_element_type=_F32)


def _pool_layer_kernel(meta_ref, x_ref, nw_ref, win_ref, wgrp_ref, scale_ref, wout_ref, out_ref, vbuf, meta_carry):
    row = pl.program_id(0)
    i = pl.program_id(1)
    weights = (nw_ref, win_ref, wgrp_ref, scale_ref, wout_ref)

    @pl.when((i == 0) & (row == 0))
    def _():
        vbuf[:, 0:POOL_CARRY, :] = jnp.zeros((len(POOL_WINDOWS), POOL_CARRY, POOL_GROUP_DIM), _F32)
        _pool_tile(lambda: meta_ref[...], True, *weights, out_ref, vbuf)
        meta_carry[...] = vbuf[:, 0:POOL_CARRY, :]

    @pl.when((i == 0) & (row > 0))
    def _():
        vbuf[:, 0:POOL_CARRY, :] = meta_carry[...]
        out_ref[0] = jnp.zeros((POOL_TILE, D_MODEL), _F32)

    @pl.when(i > 0)
    def _():
        _pool_tile(lambda: x_ref[0], False, *weights, out_ref, vbuf)


def _pool_tile(read_h, is_meta_tile, nw_ref, win_ref, wgrp_ref, scale_ref, wout_ref, out_ref, vbuf):
    hn = _rmsnorm(read_h(), nw_ref[...]).astype(_BF16)
    v = _dot(hn, win_ref[:, 0:E_POOL])
    gate = _dot(hn, win_ref[:, E_POOL:2 * E_POOL])
    for g in range(len(POOL_WINDOWS)):
        vbuf[g, POOL_CARRY:POOL_CARRY + POOL_TILE, :] = v[:, g * POOL_GROUP_DIM:(g + 1) * POOL_GROUP_DIM]

    parts = []
    for g, w in enumerate(POOL_WINDOWS):
        n_stages = w.bit_length() - 1
        acc = vbuf[g, POOL_CARRY - SUBLANES * n_stages:POOL_CARRY + POOL_TILE, :]
        for stage in range(n_stages):
            m = 1 << stage
            acc = acc[SUBLANES:] + acc[SUBLANES - m:acc.shape[0] - m]
        if is_meta_tile:
            pos = lax.broadcasted_iota(jnp.int32, (POOL_TILE, 1), 0) - POOL_PAD
            inv_cnt = 1.0 / jnp.clip(pos + 1, 1, w).astype(_F32)
        else:
            inv_cnt = 1.0 / w
        u = acc * inv_cnt - vbuf[g, POOL_CARRY:POOL_CARRY + POOL_TILE, :]
        parts.append(_dot(u.astype(_BF16), wgrp_ref[g]))
    u = jnp.concatenate(parts, axis=-1) * scale_ref[...]
    y = _dot((u * _silu(gate)).astype(_BF16), wout_ref[...])
    out_ref[0] = read_h() + y

    vbuf[:, 0:POOL_CARRY, :] = vbuf[:, POOL_TILE:POOL_TILE + POOL_CARRY, :]


def _hgrn_prologue(layer, lbl_ref, lb_s):
    logits = lbl_ref[...]
    e = jnp.exp(logits - jnp.max(logits, axis=0, keepdims=True))
    p = e / jnp.sum(e, axis=0, keepdims=True)
    lb_s[...] = jnp.sum(p[0:layer + 1], axis=0, keepdims=True) - p[0:1]


def _hgrn_project(h_ref, nw_ref, win_ref, hn_s, q_s, g_s):
    hn = _rmsnorm(h_ref[0], nw_ref[...]).astype(_BF16)
    hn_s[...] = hn
    q_s[...] = _silu(_dot(hn, win_ref[:, 0:E_HGRN]).astype(_BF16))
    g_s[...] = _dot(hn, win_ref[:, E_HGRN:2 * E_HGRN])


def _hgrn_fast_kernel(layer, h_ref, nw_ref, win_ref, lbl_ref, onw_ref, wout_ref, fw_ref, out_ref, growth_ref,
                      hn_s, lb_s, q_s, k_s, v_s, g_s, gate_s, qin_s, kin_s, a_s, o_s, st_s, meta_state):
    row = pl.program_id(0)
    i = pl.program_id(1)
    reuse_meta = (i == 0) & (row > 0)

    @pl.when((i == 0) & (row == 0))
    def _():
        _hgrn_prologue(layer, lbl_ref, lb_s)

    @pl.when(reuse_meta)
    def _():
        st_s[...] = meta_state[...]
        growth_ref[...] = jnp.zeros(growth_ref.shape, _F32)

    @pl.when(jnp.logical_not(reuse_meta))
    def _():
        @pl.when(i == 0)
        def _():
            st_s[...] = jnp.zeros(st_s.shape, _F32)

        def store_growth(c, value):
            growth_ref[0, 0, c:c + 1, :] = value

        _hgrn_project(h_ref, nw_ref, win_ref, hn_s, q_s, g_s)
        _hgrn_mix(win_ref, hn_s, lb_s, q_s, k_s, v_s, g_s, gate_s, qin_s, kin_s, a_s, o_s, st_s, store_growth)
        _hgrn_output(h_ref, onw_ref, wout_ref, fw_ref, out_ref, gate_s, o_s)

        @pl.when(i == 0)
        def _():
            meta_state[...] = st_s[...]


def _hgrn_safe_kernel(layer, h_ref, nw_ref, win_ref, lbl_ref, onw_ref, wout_ref, fw_ref, out_ref,
                      hn_s, lb_s, q_s, k_s, v_s, g_s, gate_s, qin_s, kin_s, a_s, o_s, st_s, st0_s, growth_s,
                      meta_state):
    row = pl.program_id(0)
    i = pl.program_id(1)
    reuse_meta = (i == 0) & (row > 0)

    @pl.when((i == 0) & (row == 0))
    def _():
        _hgrn_prologue(layer, lbl_ref, lb_s)

    @pl.when(reuse_meta)
    def _():
        st_s[...] = meta_state[...]
        st0_s[...] = meta_state[...]

    @pl.when(jnp.logical_not(reuse_meta))
    def _():
        @pl.when(i == 0)
        def _():
            st_s[...] = jnp.zeros(st_s.shape, _F32)
            st0_s[...] = jnp.zeros(st_s.shape, _F32)

        _hgrn_project(h_ref, nw_ref, win_ref, hn_s, q_s, g_s)

    @pl.when((i > 0) | (row == 0))
    def _():
        def store_growth(c, value):
            growth_s[c:c + 1, :] = value

        _hgrn_mix(win_ref, hn_s, lb_s, q_s, k_s, v_s, g_s, gate_s, qin_s, kin_s, a_s, o_s, st_s, store_growth)
        _hgrn_output(h_ref, onw_ref, wout_ref, fw_ref, out_ref, gate_s, o_s)

        in_range = jnp.max(growth_s[...]) <= MAX_LOG2_GROWTH

        @pl.when(jnp.logical_not(in_range))
        def _():
            st_s[...] = st0_s[...]
            _hgrn_recurrence_small_steps(q_s, k_s, v_s, g_s, o_s, st_s)
            _hgrn_output(h_ref, onw_ref, wout_ref, fw_ref, out_ref, gate_s, o_s)

        st0_s[...] = st_s[...]

        @pl.when(i == 0)
        def _():
            meta_state[...] = st_s[...]


def _store_heads(ref, rows, value):
    for hd in range(N_HEADS):
        ref[hd, rows] = value[:, hd * HEAD_DIM:(hd + 1) * HEAD_DIM]


def _hgrn_mix(win_ref, hn_s, lb_s, q_s, k_s, v_s, g_s, gate_s, qin_s, kin_s, a_s, o_s, st_s, store_growth):
    lb = lb_s[...]
    hn = hn_s[...]
    _store_heads(v_s, slice(None), _dot(hn, win_ref[:, 2 * E_HGRN:3 * E_HGRN]).astype(_BF16))
    gate_s[...] = _silu(_dot(hn, win_ref[:, 3 * E_HGRN:4 * E_HGRN]).astype(_BF16))

    fp = g_s[...]
    t = jnp.exp2(jnp.abs(fp) * (-LOG2_E))
    r = 1.0 / (1.0 + t)
    tr = t * r
    sig_pos = jnp.where(fp >= 0, r, tr)
    sig_neg = jnp.where(fp >= 0, tr, r)
    g_s[...] = jnp.log2(lb + (1.0 - lb) * sig_pos)
    k_s[...] = (1.0 - lb) * sig_neg

    row = lax.broadcasted_iota(jnp.int32, (CHUNK, CHUNK), 0)
    col = lax.broadcasted_iota(jnp.int32, (CHUNK, CHUNK), 1)
    causal = col <= row
    tri = causal.astype(_BF16)
    tri2 = jnp.concatenate([tri, tri], axis=1)

    d_start_mid, d_mid_end, d_start_end = [], [], []
    n_chunks = q_s.shape[0] // CHUNK
    for c in range(n_chunks):
        rs = slice(c * CHUNK, (c + 1) * CHUNK)
        g = g_s[rs, :]
        g_hi = g.astype(_BF16)
        g_lo = (g - g_hi.astype(_F32)).astype(_BF16)
        b = _dot(tri2, jnp.concatenate([g_hi, g_lo], axis=0))
        b_mid = b[HALF - 1:HALF, :]
        b_end = b[CHUNK - 1:CHUNK, :]
        _store_heads(qin_s, rs, q_s[rs, :] * jnp.exp2(b - b_mid).astype(_BF16))
        _store_heads(kin_s, rs, (k_s[rs, :] * jnp.exp2(b_mid - b)).astype(_BF16))
        d_start_mid.append(jnp.exp2(b_mid))
        d_mid_end.append(jnp.exp2(b_end - b_mid))
        d_start_end.append(jnp.exp2(b_end))
        store_growth(c, jnp.maximum(b[0:1, :] - b_mid, b_mid - b_end))

    for c in range(n_chunks):
        rs = slice(c * CHUNK, (c + 1) * CHUNK)
        for hd in range(N_HEADS):
            ls = slice(hd * HEAD_DIM, (hd + 1) * HEAD_DIM)
            a = _dot_nt(qin_s[hd, rs], kin_s[hd, rs])
            a_s[hd, rs] = jnp.where(causal, a, 0.0).astype(_BF16)

    for c in range(n_chunks):
        rs = slice(c * CHUNK, (c + 1) * CHUNK)
        for hd in range(N_HEADS):
            ls = slice(hd * HEAD_DIM, (hd + 1) * HEAD_DIM)
            st = st_s[hd]
            v = v_s[hd, rs]
            k_mid = kin_s[hd, rs]
            s_mid = jnp.transpose(st * d_start_mid[c][:, ls]).astype(_BF16)
            lhs = jnp.concatenate([qin_s[hd, rs], a_s[hd, rs]], axis=1)
            rhs = jnp.concatenate([s_mid, v], axis=0)
            o_s[hd, rs] = _dot(lhs, rhs)
            st_s[hd] = st * d_start_end[c][:, ls] + _dot_tn(v, k_mid) * d_mid_end[c][:, ls]


def _hgrn_output(h_ref, onw_ref, wout_ref, fw_ref, out_ref, gate_s, o_s):
    parts = []
    for hd in range(N_HEADS):
        ls = slice(hd * HEAD_DIM, (hd + 1) * HEAD_DIM)
        o = o_s[hd]
        parts.append(o * lax.rsqrt(jnp.mean(o * o, axis=-1, keepdims=True) + EPS))
    o = jnp.concatenate(parts, axis=-1) * onw_ref[...]
    y = _dot(o.astype(_BF16) * gate_s[...], wout_ref[...])
    out_ref[0] = _rmsnorm(h_ref[0] + y, fw_ref[...])


def _hgrn_recurrence_small_steps(q_s, k_s, v_s, g_s, o_s, st_s):
    step_row = lax.broadcasted_iota(jnp.int32, (SMALL_STEP, 1), 0)
    r_i = lax.broadcasted_iota(jnp.int32, (SMALL_STEP, SMALL_STEP), 0)
    c_i = lax.broadcasted_iota(jnp.int32, (SMALL_STEP, SMALL_STEP), 1)
    tri = (c_i <= r_i).astype(_BF16)
    tri3 = jnp.concatenate([tri, tri, tri], axis=1)

    def body(j, carry):
        rs = pl.ds(pl.multiple_of(j * SMALL_STEP, SMALL_STEP), SMALL_STEP)
        for hd in range(N_HEADS):
            ls = slice(hd * HEAD_DIM, (hd + 1) * HEAD_DIM)
            g = g_s[rs, ls]
            g_hi = g.astype(_BF16)
            g_md = (g - g_hi.astype(_F32)).astype(_BF16)
            g_lo = (g - g_hi.astype(_F32) - g_md.astype(_F32)).astype(_BF16)
            b = _dot(tri3, jnp.concatenate([g_hi, g_md, g_lo], axis=0))
            b_end = b[SMALL_STEP - 1:SMALL_STEP, :]
            q = q_s[rs, ls].astype(_F32)
            k = k_s[rs, ls]
            v = v_s[hd, rs].astype(_F32)
            st = st_s[hd]
            o = _dot_nt((q * jnp.exp2(b)).astype(_BF16), st.astype(_BF16))
            rows = []
            for t in range(SMALL_STEP):
                rel = jnp.where(step_row <= t, b[t:t + 1, :] - b, -jnp.inf)
                score = jnp.sum(q[t:t + 1, :] * k * jnp.exp2(rel), axis=-1, keepdims=True)
                rows.append(jnp.sum(score * v, axis=0, keepdims=True))
            o_s[hd, rs] = o + jnp.concatenate(rows, axis=0)
            k_end = (k * jnp.exp2(b_end - b)).astype(_BF16)
            st_s[hd] = st * jnp.exp2(b_end) + _dot_tn(v.astype(_BF16), k_end)
        return carry

    lax.fori_loop(0, q_s.shape[0] // SMALL_STEP, body, 0)


def _const_spec(shape):
    zeros = (0,) * len(shape)
    return pl.BlockSpec(shape, lambda b, i: zeros, pipeline_mode=pl.Buffered(1))


def kernel(x, meta_tokens, norm_w, pool_w_in, pool_w_grp, pool_scale, pool_w_out, hgrn_w_in, hgrn_lb_logits,
           hgrn_o_norm, hgrn_w_out, final_norm_w):
    batch, seq, d = x.shape
    depth = norm_w.shape[0]
    assert d == D_MODEL and seq % POOL_TILE == 0 and depth == 2
    assert POOL_TILE % HGRN_TILE == 0 and POOL_TILE % HGRN_ANY_RANGE_TILE == 0
    assert meta_tokens.shape == (N_META, D_MODEL) and N_META >= MAX_WINDOW - 1
    params = pltpu.CompilerParams(dimension_semantics=("arbitrary", "arbitrary"),
                                  vmem_limit_bytes=VMEM_LIMIT_BYTES)

    meta_pad = jnp.concatenate([jnp.zeros((POOL_PAD, D_MODEL), x.dtype), meta_tokens.astype(x.dtype)], axis=0)
    pool_tiles = seq // POOL_TILE + 1

    def tile_spec(rows, first):
        return pl.BlockSpec((1, rows, D_MODEL), lambda b, i: (b, jnp.maximum(i + first, 0), 0))

    h1 = pl.pallas_call(
        _pool_layer_kernel,
        grid=(batch, pool_tiles),
        in_specs=[
            _const_spec((POOL_TILE, D_MODEL)),
            tile_spec(POOL_TILE, -1),
            _const_spec((1, D_MODEL)),
            _const_spec((D_MODEL, 2 * E_POOL)),
            _const_spec((len(POOL_WINDOWS), POOL_GROUP_DIM, POOL_GROUP_DIM)),
            _const_spec((1, E_POOL)),
            _const_spec((E_POOL, D_MODEL)),
        ],
        out_specs=tile_spec(POOL_TILE, 0),
        out_shape=jax.ShapeDtypeStruct((batch, pool_tiles * POOL_TILE, D_MODEL), _F32),
        scratch_shapes=[
            pltpu.VMEM((len(POOL_WINDOWS), POOL_TILE + POOL_CARRY, POOL_GROUP_DIM), _F32),
            pltpu.VMEM((len(POOL_WINDOWS), POOL_CARRY, POOL_GROUP_DIM), _F32),
        ],
        compiler_params=params,
        name="pool_layer",
    )(meta_pad, x, norm_w[0:1], pool_w_in[0].astype(_BF16), pool_w_grp[0].astype(_BF16), pool_scale[0:1],
      pool_w_out[0].astype(_BF16))

    state = pltpu.VMEM((N_HEADS, HEAD_DIM, HEAD_DIM), _F32)
    hgrn_operands = (h1, norm_w[1:2], hgrn_w_in[0].astype(_BF16), hgrn_lb_logits, hgrn_o_norm[0:1],
                     hgrn_w_out[0].astype(_BF16), final_norm_w.reshape(1, D_MODEL))
    out_struct = jax.ShapeDtypeStruct((batch, seq, D_MODEL), _F32)

    def hgrn_call(kernel_fn, rows, name, reports_growth, extra_scratch):
        n_tiles = seq // rows + 1
        n_chunks = rows // CHUNK
        tile_f32 = pltpu.VMEM((rows, E_HGRN), _F32)
        tile_bf16 = pltpu.VMEM((rows, E_HGRN), _BF16)
        heads_bf16 = pltpu.VMEM((N_HEADS, rows, HEAD_DIM), _BF16)
        out_specs, out_shape = tile_spec(rows, -1), out_struct
        if reports_growth:
            out_specs = [out_specs, pl.BlockSpec((1, 1, n_chunks, E_HGRN), lambda b, i: (b, i, 0, 0))]
            out_shape = [out_shape, jax.ShapeDtypeStruct((batch, n_tiles, n_chunks, E_HGRN), _F32)]
        return pl.pallas_call(
            functools.partial(kernel_fn, 1),
            grid=(batch, n_tiles),
            in_specs=[
                tile_spec(rows, POOL_TILE // rows - 1),
                _const_spec((1, D_MODEL)),
                _const_spec((D_MODEL, 4 * E_HGRN)),
                _const_spec((depth, E_HGRN)),
                _const_spec((1, E_HGRN)),
                _const_spec((E_HGRN, D_MODEL)),
                _const_spec((1, D_MODEL)),
            ],
            out_specs=out_specs,
            out_shape=out_shape,
            scratch_shapes=[
                pltpu.VMEM((rows, D_MODEL), _BF16),
                pltpu.VMEM((1, E_HGRN), _F32),
                tile_bf16,
                tile_f32,
                heads_bf16,
                tile_f32,
                tile_bf16,
                heads_bf16,
                heads_bf16,
                heads_bf16,
                pltpu.VMEM((N_HEADS, rows, HEAD_DIM), _F32),
                state,
            ] + extra_scratch(n_chunks),
            compiler_params=params,
            name=name,
        )(*hgrn_operands)

    out_fast, growth = hgrn_call(_hgrn_fast_kernel, HGRN_TILE, "hgrn_layer", True,
                                 lambda n_chunks: [state])

    def hgrn_any_range():
        return hgrn_call(_hgrn_safe_kernel, HGRN_ANY_RANGE_TILE, "hgrn_layer_any_range", False, lambda n_chunks: [
            state,
            pltpu.VMEM((n_chunks, E_HGRN), _F32),
            state,
        ])

    in_range = jnp.max(growth) <= MAX_LOG2_GROWTH
    return lax.cond(in_range, lambda: out_fast, hgrn_any_range)
```
